```python
import jax, jax.numpy as jnp
from jax import lax
import numpy as np

D_MODEL = 1024
BATCH = 8
SEQ = 2048
DEPTH = 4
DEC_BATCH = 128
DEC_SEQ = 4
PAST_LEN = 16384
PAGE_SIZE = 128

N_AB_LAYERS = (DEPTH + 1) // 2
N_C_LAYERS = DEPTH // 2
H_A = 8
N_A = 64
C_A = H_A * N_A
W_LORA = 64
A_LORA = 64
G_LORA = 128
A_SHIFT_COLS = 3 * C_A + W_LORA + A_LORA + G_LORA
RWKV_GN_EPS = 64e-5
H_B = 4
DK_B = 128
DV_B = 128
C_B = H_B * DV_B
RET_CHUNK = 128
ROPE_THETA = 10000.0
RET_GN_EPS = 1e-5
AB_COLS = A_SHIFT_COLS + 2 * H_B * DK_B + 2 * C_B
H_C = 8
DK_C = 128
DV_C = 128
C_C = H_C * DV_C
CONV_W = 4
GDN_CHUNK = 64
N_QKV_C = 2 * H_C * DK_C + C_C
C_COLS = N_QKV_C + 2 * H_C + C_C
D_FF = 2816
LN_EPS = 1e-5
DEEPNORM_ALPHA = (2 * DEPTH) ** 0.25
DEEPNORM_BETA = (8 * DEPTH) ** -0.25

kernel_name = 'rwkv7_retnet_gdn_macaron_deepnorm_step'


def layer_norm(x, g, b):
    xf = x.astype(jnp.float32)
    xc = xf - jnp.mean(xf, -1, keepdims=True)
    var = jnp.mean(xc * xc, -1, keepdims=True)
    return xc * lax.rsqrt(var + LN_EPS) * g + b


def group_norm(x, g, b, eps):
    xf = x.astype(jnp.float32)
    xc = xf - jnp.mean(xf, -1, keepdims=True)
    var = jnp.mean(xc * xc, -1, keepdims=True)
    shp = x.shape[-2:]
    return xc * lax.rsqrt(var + eps) * g.reshape(shp) + b.reshape(shp)


def l2_normalize(x):
    xf = x.astype(jnp.float32)
    return xf * lax.rsqrt(jnp.maximum(jnp.sum(xf * xf, -1, keepdims=True), 1e-12))


def swiglu(x, wg, wu, wd):
    return (jax.nn.silu(x @ wg) * (x @ wu)) @ wd


def rotary(x, pos):
    half = x.shape[-1] // 2
    inv_freq = 1.0 / (ROPE_THETA ** jnp.linspace(0.0, 1.0, half, dtype=jnp.float32))
    ang = pos[:, None] * inv_freq[None, :]
    cos = jnp.cos(ang)[None, :, None, :]
    sin = jnp.sin(ang)[None, :, None, :]
    x1, x2 = x[..., :half], x[..., half:]
    return jnp.concatenate([x1 * cos - x2 * sin, x2 * cos + x1 * sin], -1)


def retention_log_gamma():
    return jnp.log(1.0 - jnp.exp2(-5.0 - jnp.arange(H_B, dtype=jnp.float32)))


def rwkv7_recurrence(r, decay, k, v, kk, a, S0):
    def step(S, inp):
        r_t, w_t, k_t, v_t, kk_t, a_t = inp
        sk = jnp.einsum('bhvk,bhk->bhv', S, kk_t)
        S = (S * w_t[:, :, None, :] - sk[..., None] * (kk_t * a_t)[:, :, None, :]
             + v_t[..., None] * k_t[:, :, None, :])
        return S, jnp.einsum('bhvk,bhk->bhv', S, r_t)
    xs = tuple(jnp.moveaxis(t.astype(jnp.float32), 1, 0) for t in (r, decay, k, v, kk, a))
    S, o = lax.scan(step, S0.astype(jnp.float32), xs)
    return jnp.moveaxis(o, 0, 1), S


def retention_chunked(q, k, v, S0):
    Bn, L, H, DK = q.shape
    DV = v.shape[-1]
    C = RET_CHUNK if L % RET_CHUNK == 0 else L
    n = L // C
    blk = lambda t: t.astype(jnp.float32).reshape(Bn, n, C, H, t.shape[-1]).transpose(1, 0, 3, 2, 4)
    q, k, v = blk(q), blk(k), blk(v)
    lg = retention_log_gamma()
    idx = jnp.arange(C, dtype=jnp.float32)
    rel = idx[:, None] - idx[None, :]
    causal = rel >= 0
    dmat = jnp.where(causal, jnp.exp(jnp.where(causal, rel, 0.0)[None] * lg[:, None, None]), 0.0)
    xi = jnp.exp((idx[None, :] + 1.0) * lg[:, None])[..., None]
    zeta = jnp.exp((C - 1.0 - idx[None, :]) * lg[:, None])[..., None]
    g_chunk = jnp.exp(C * lg)[:, None, None]
    inner = (jnp.einsum('nbhid,nbhjd->nbhij', q, k) * dmat) @ v
    kz = k * zeta

    def step(S, inp):
        q_c, kz_c, v_c = inp
        o = (q_c @ S) * xi
        S = g_chunk * S + jnp.swapaxes(kz_c, -1, -2) @ v_c
        return S, o
    S, cross = lax.scan(step, S0.astype(jnp.float32), (q, kz, v))
    o = (inner + cross).transpose(1, 0, 3, 2, 4).reshape(Bn, L, H, DV)
    return o, S


def gated_delta_chunked(q, k, v, g, beta, S0):
    Bn, L, H, DK = q.shape
    DV = v.shape[-1]
    C = GDN_CHUNK if L % GDN_CHUNK == 0 else L
    n = L // C
    blk4 = lambda t: t.astype(jnp.float32).reshape(Bn, n, C, H, t.shape[-1]).transpose(1, 0, 3, 2, 4)
    blk3 = lambda t: t.astype(jnp.float32).reshape(Bn, n, C, H).transpose(1, 0, 3, 2)
    q = blk4(q) * DK ** -0.5
    k, v = blk4(k), blk4(v)
    g, beta = blk3(g), blk3(beta)
    gc = jnp.cumsum(g, -1)
    idx = jnp.arange(C)
    incl = idx[:, None] >= idx[None, :]
    strict = idx[:, None] > idx[None, :]
    diff = gc[..., :, None] - gc[..., None, :]
    dmat = jnp.where(incl, jnp.exp(jnp.where(incl, diff, 0.0)), 0.0)
    kb = k * beta[..., None]
    A = jnp.where(strict, jnp.einsum('nbhid,nbhjd->nbhij', kb, k) * dmat, 0.0)
    eye = jnp.eye(C, dtype=jnp.float32)
    T = lax.linalg.triangular_solve(A + eye, jnp.broadcast_to(eye, A.shape),
                                    left_side=True, lower=True, unit_diagonal=True)
    u = T @ (v * beta[..., None])
    w = T @ (kb * jnp.exp(gc)[..., None])
    qk = jnp.einsum('nbhid,nbhjd->nbhij', q, k) * dmat
    qg = q * jnp.exp(gc)[..., None]
    kd = k * jnp.exp(gc[..., -1:] - gc)[..., None]
    g_last = jnp.exp(gc[..., -1])[..., None, None]

    def step(S, inp):
        u_c, w_c, qk_c, qg_c, kd_c, gl_c = inp
        v_new = u_c - w_c @ S
        o = qg_c @ S + qk_c @ v_new
        S = gl_c * S + jnp.swapaxes(kd_c, -1, -2) @ v_new
        return S, o
    S, o = lax.scan(step, S0.astype(jnp.float32), (u, w, qk, qg, kd, g_last))
    return o.transpose(1, 0, 3, 2, 4).reshape(Bn, L, H, DV), S


def ab_mixer(h, pos, shift_prev, S_rwkv0, S_ret0, p, i):
    Bn, L, _ = h.shape
    z = (h @ p['ab_w_in'][i]).astype(jnp.float32)
    za, zb = z[..., :A_SHIFT_COLS], z[..., A_SHIFT_COLS:]
    za_prev = jnp.concatenate([shift_prev.astype(jnp.float32)[:, None, :], za[:, :-1]], axis=1)
    zs = za + (za_prev - za) * p['rwkv_mu'][i]
    r, k, v, wd, ad, gd = jnp.split(
        zs, (C_A, 2 * C_A, 3 * C_A, 3 * C_A + W_LORA, 3 * C_A + W_LORA + A_LORA), axis=-1)
    w_log = -jax.nn.softplus(-(p['rwkv_w0'][i] + jnp.tanh(wd) @ p['rwkv_w2'][i])) - 0.5
    decay = jnp.exp(-jnp.exp(w_log))
    a = jax.nn.sigmoid(p['rwkv_a0'][i] + ad @ p['rwkv_a2'][i])
    gate = jax.nn.sigmoid(gd) @ p['rwkv_g2'][i]
    split_a = lambda t: t.reshape(Bn, L, H_A, N_A)
    kk = l2_normalize(split_a(k * p['rwkv_k_k'][i]))
    k = k * (1.0 + (a - 1.0) * p['rwkv_k_a'][i])
    r_h, k_h, v_h, a_h = split_a(r), split_a(k), split_a(v), split_a(a)
    o_a, S_rwkv = rwkv7_recurrence(r_h, split_a(decay), k_h, v_h, kk, a_h, S_rwkv0)
    o_a = (group_norm(o_a, p['rwkv_gn_g'][i], p['rwkv_gn_b'][i], RWKV_GN_EPS)
           + jnp.sum(r_h * k_h * p['rwkv_r_k'][i], -1, keepdims=True) * v_h)
    o_a = o_a.reshape(Bn, L, C_A) * gate
    qr, kr, vr, gr = jnp.split(zb, (H_B * DK_B, 2 * H_B * DK_B, 2 * H_B * DK_B + C_B), axis=-1)
    qr = rotary(qr.reshape(Bn, L, H_B, DK_B), pos)
    kr = rotary(kr.reshape(Bn, L, H_B, DK_B), pos) * DK_B ** -0.5
    o_b, S_ret = retention_chunked(qr, kr, vr.reshape(Bn, L, H_B, DV_B), S_ret0)
    o_b = group_norm(o_b, p['ret_gn_g'][i], p['ret_gn_b'][i], RET_GN_EPS).reshape(Bn, L, C_B) * jax.nn.silu(gr)
    out = jnp.concatenate([o_a, o_b], -1) @ p['ab_w_out'][i]
    return out, za[:, -1], S_rwkv, S_ret


def gdn_mixer(h, conv_prev, S0, p, i):
    Bn, L, _ = h.shape
    z = (h @ p['gdn_w_in'][i]).astype(jnp.float32)
    qkv, b_in, a_in, zg = jnp.split(z, (N_QKV_C, N_QKV_C + H_C, N_QKV_C + 2 * H_C), axis=-1)
    xp = jnp.concatenate([conv_prev.astype(jnp.float32), qkv], axis=1)
    cw = p['gdn_conv_w'][i]
    conv = xp[:, 0:L] * cw[0]
    for j in range(1, CONV_W):
        conv = conv + xp[:, j:j + L] * cw[j]
    qkv = jax.nn.silu(conv)
    q, k, v = jnp.split(qkv, (H_C * DK_C, 2 * H_C * DK_C), axis=-1)
    q = l2_normalize(q.reshape(Bn, L, H_C, DK_C))
    k = l2_normalize(k.reshape(Bn, L, H_C, DK_C))
    v = v.reshape(Bn, L, H_C, DV_C)
    beta = jax.nn.sigmoid(b_in)
    g = -jnp.exp(p['gdn_A_log'][i]) * jax.nn.softplus(a_in + p['gdn_dt_bias'][i])
    o, S = gated_delta_chunked(q, k, v, g, beta, S0)
    o = (o * lax.rsqrt(jnp.mean(o * o, -1, keepdims=True) + 1e-6) * p['gdn_norm_g'][i]
         * jax.nn.silu(zg.reshape(Bn, L, H_C, DV_C)))
    return o.reshape(Bn, L, C_C) @ p['gdn_w_out'][i], xp[:, L:], S


def trunk(x, pos0, states, p):
    st_shift, st_rwkv, st_ret, st_conv, st_gdn = states
    pos = pos0 + jnp.arange(x.shape[1], dtype=jnp.float32)
    new_shift, new_rwkv, new_ret, new_conv, new_gdn = [], [], [], [], []
    h = x
    for l in range(DEPTH):
        i = l // 2
        f = swiglu(h, p['ffn_w_gate'][l, 0], p['ffn_w_up'][l, 0], p['ffn_w_down'][l, 0])
        h = layer_norm(DEEPNORM_ALPHA * h + 0.5 * f, p['ln_g'][l, 0], p['ln_b'][l, 0])
        if l % 2 == 0:
            m, s_sh, s_rw, s_rt = ab_mixer(h, pos, st_shift[i], st_rwkv[i], st_ret[i], p, i)
            new_shift.append(s_sh)
            new_rwkv.append(s_rw)
            new_ret.append(s_rt)
        else:
            m, s_cv, s_gd = gdn_mixer(h, st_conv[i], st_gdn[i], p, i)
            new_conv.append(s_cv)
            new_gdn.append(s_gd)
        h = layer_norm(DEEPNORM_ALPHA * h + m, p['ln_g'][l, 1], p['ln_b'][l, 1])
        f = swiglu(h, p['ffn_w_gate'][l, 1], p['ffn_w_up'][l, 1], p['ffn_w_down'][l, 1])
        h = layer_norm(DEEPNORM_ALPHA * h + 0.5 * f, p['ln_g'][l, 2], p['ln_b'][l, 2])
    return h, (jnp.stack(new_shift), jnp.stack(new_rwkv), jnp.stack(new_ret),
               jnp.stack(new_conv), jnp.stack(new_gdn))


def setup_inputs(seed: int = 0) -> dict:
    key = jax.random.key(seed)
    ks = iter(jax.random.split(key, 40))
    f32 = jnp.float32
    nrm = lambda shape, scale: jax.random.normal(next(ks), shape, f32) * scale
    uni = lambda shape, lo, hi: jax.random.uniform(next(ks), shape, f32, lo, hi)
    d = {}
    d['x_prompt'] = nrm((BATCH, SEQ, D_MODEL), 1.0)
    d['x_sample'] = nrm((DEC_BATCH, DEC_SEQ, D_MODEL), 1.0)
    d['state_rwkv_shift'] = nrm((N_AB_LAYERS, DEC_BATCH, A_SHIFT_COLS), 1.0)
    d['state_rwkv'] = nrm((N_AB_LAYERS, DEC_BATCH, H_A, N_A, N_A), 0.3)
    d['state_ret'] = nrm((N_AB_LAYERS, DEC_BATCH, H_B, DK_B, DV_B), 0.3)
    d['state_gdn_conv'] = nrm((N_C_LAYERS, DEC_BATCH, CONV_W - 1, N_QKV_C), 1.0)
    d['state_gdn'] = nrm((N_C_LAYERS, DEC_BATCH, H_C, DK_C, DV_C), 0.3)
    d['ln_g'] = 1.0 + nrm((DEPTH, 3, D_MODEL), 0.02)
    d['ln_b'] = nrm((DEPTH, 3, D_MODEL), 0.02)
    d['ffn_w_gate'] = nrm((DEPTH, 2, D_MODEL, D_FF), D_MODEL ** -0.5)
    d['ffn_w_up'] = nrm((DEPTH, 2, D_MODEL, D_FF), D_MODEL ** -0.5)
    d['ffn_w_down'] = nrm((DEPTH, 2, D_FF, D_MODEL), D_FF ** -0.5 * DEEPNORM_BETA)
    d['ab_w_in'] = nrm((N_AB_LAYERS, D_MODEL, AB_COLS), D_MODEL ** -0.5)
    d['ab_w_out'] = nrm((N_AB_LAYERS, C_A + C_B, D_MODEL), (C_A + C_B) ** -0.5 * DEEPNORM_BETA)
    d['rwkv_mu'] = uni((N_AB_LAYERS, A_SHIFT_COLS), 0.0, 1.0)
    d['rwkv_w0'] = jnp.linspace(-5.0, 1.0, C_A, dtype=f32)[None, :] + nrm((N_AB_LAYERS, C_A), 0.1)
    d['rwkv_w2'] = nrm((N_AB_LAYERS, W_LORA, C_A), W_LORA ** -0.5 * 0.3)
    d['rwkv_a0'] = nrm((N_AB_LAYERS, C_A), 0.3)
    d['rwkv_a2'] = nrm((N_AB_LAYERS, A_LORA, C_A), A_LORA ** -0.5 * 0.3)
    d['rwkv_g2'] = nrm((N_AB_LAYERS, G_LORA, C_A), G_LORA ** -0.5)
    d['rwkv_k_k'] = 0.85 + nrm((N_AB_LAYERS, C_A), 0.05)
    d['rwkv_k_a'] = 1.0 + nrm((N_AB_LAYERS, C_A), 0.05)
    d['rwkv_r_k'] = nrm((N_AB_LAYERS, H_A, N_A), 0.1)
    d['rwkv_gn_g'] = 1.0 + nrm((N_AB_LAYERS, C_A), 0.02)
    d['rwkv_gn_b'] = nrm((N_AB_LAYERS, C_A), 0.02)
    d['ret_gn_g'] = 1.0 + nrm((N_AB_LAYERS, C_B), 0.02)
    d['ret_gn_b'] = nrm((N_AB_LAYERS, C_B), 0.02)
    d['gdn_w_in'] = nrm((N_C_LAYERS, D_MODEL, C_COLS), D_MODEL ** -0.5)
    d['gdn_conv_w'] = nrm((N_C_LAYERS, CONV_W, N_QKV_C), 0.5)
    d['gdn_A_log'] = jnp.log(uni((N_C_LAYERS, H_C), 1.0, 16.0))
    dt = jnp.exp(uni((N_C_LAYERS, H_C), float(np.log(1e-3)), float(np.log(1e-1))))
    d['gdn_dt_bias'] = dt + jnp.log(-jnp.expm1(-dt))
    d['gdn_norm_g'] = 1.0 + nrm((N_C_LAYERS, DV_C), 0.02)
    d['gdn_w_out'] = nrm((N_C_LAYERS, C_C, D_MODEL), C_C ** -0.5 * DEEPNORM_BETA)
    return d


def reference(x_prompt, x_sample, state_rwkv_shift, state_rwkv, state_ret, state_gdn_conv, state_gdn,
              ln_g, ln_b, ffn_w_gate, ffn_w_up, ffn_w_down, ab_w_in, ab_w_out, rwkv_mu, rwkv_w0,
              rwkv_w2, rwkv_a0, rwkv_a2, rwkv_g2, rwkv_k_k, rwkv_k_a, rwkv_r_k, rwkv_gn_g, rwkv_gn_b,
              ret_gn_g, ret_gn_b, gdn_w_in, gdn_conv_w, gdn_A_log, gdn_dt_bias, gdn_norm_g, gdn_w_out):
    p = dict(ln_g=ln_g, ln_b=ln_b, ffn_w_gate=ffn_w_gate, ffn_w_up=ffn_w_up, ffn_w_down=ffn_w_down,
             ab_w_in=ab_w_in, ab_w_out=ab_w_out, rwkv_mu=rwkv_mu, rwkv_w0=rwkv_w0, rwkv_w2=rwkv_w2,
             rwkv_a0=rwkv_a0, rwkv_a2=rwkv_a2, rwkv_g2=rwkv_g2, rwkv_k_k=rwkv_k_k, rwkv_k_a=rwkv_k_a,
             rwkv_r_k=rwkv_r_k, rwkv_gn_g=rwkv_gn_g, rwkv_gn_b=rwkv_gn_b, ret_gn_g=ret_gn_g,
             ret_gn_b=ret_gn_b, gdn_w_in=gdn_w_in, gdn_conv_w=gdn_conv_w, gdn_A_log=gdn_A_log,
             gdn_dt_bias=gdn_dt_bias, gdn_norm_g=gdn_norm_g, gdn_w_out=gdn_w_out)
    nb = x_prompt.shape[0]
    zeros = lambda shape: jnp.zeros(shape, jnp.float32)
    empty = (zeros((N_AB_LAYERS, nb, A_SHIFT_COLS)), zeros((N_AB_LAYERS, nb, H_A, N_A, N_A)),
             zeros((N_AB_LAYERS, nb, H_B, DK_B, DV_B)), zeros((N_C_LAYERS, nb, CONV_W - 1, N_QKV_C)),
             zeros((N_C_LAYERS, nb, H_C, DK_C, DV_C)))
    y_prompt, (p_shift, p_rwkv, p_ret, p_conv, p_gdn) = trunk(x_prompt, 0, empty, p)
    past = (state_rwkv_shift, state_rwkv, state_ret, state_gdn_conv, state_gdn)
    y_sample, (s_shift, s_rwkv, s_ret, s_conv, s_gdn) = trunk(x_sample, PAST_LEN, past, p)
    return (y_prompt, y_sample, p_shift, p_rwkv, p_ret, p_conv, p_gdn, s_shift, s_rwkv, s_ret, s_conv, s_gdn)
```

```python
import functools
import math

import numpy as np
import jax
import jax.numpy as jnp
from jax import lax
from jax.experimental import pallas as pl
from jax.experimental.pallas import tpu as pltpu

F32 = jnp.float32
BF16 = jnp.bfloat16

D_MODEL = 1024
DEPTH = 4
PAST_LEN = 16384
N_AB_LAYERS = (DEPTH + 1) // 2
N_C_LAYERS = DEPTH // 2
H_A, N_A = 8, 64
C_A = H_A * N_A
W_LORA, A_LORA, G_LORA = 64, 64, 128
LORA_COLS = W_LORA + A_LORA + G_LORA
A_SHIFT_COLS = 3 * C_A + LORA_COLS
RWKV_GN_EPS = 64e-5
H_B, DK_B, DV_B = 4, 128, 128
C_B = H_B * DV_B
ROPE_THETA = 10000.0
RET_GN_EPS = 1e-5
B_COLS = 2 * H_B * DK_B + 2 * C_B
AB_COLS = A_SHIFT_COLS + B_COLS
H_C, DK_C, DV_C = 8, 128, 128
C_C = H_C * DV_C
CONV_W = 4
N_QKV_C = 2 * H_C * DK_C + C_C
D_FF = 2816
LN_EPS = 1e-5
DEEPNORM_ALPHA = (2 * DEPTH) ** 0.25

V7X_VMEM_BYTES = 64 * 1024 * 1024
LANES = 128
SUBLANES = 8
VMEM_LIMIT = V7X_VMEM_BYTES * 7 // 8

SAMPLE_PAD = SUBLANES
SEQ_PER_STEP = 8
ROW_TILE = 512
RWKV_GROUP = 4
GDN_GROUP = 4


class SeqCfg:
    def __init__(self, nseq, lb, valid, nc, pos0, has_state):
        self.nseq, self.lb, self.valid, self.nc, self.pos0, self.has_state = nseq, lb, valid, nc, pos0, has_state
        self.rows = nseq * lb
        self.shift = int(math.log2(lb))
        assert 1 << self.shift == lb and valid >= CONV_W - 1


def _cparams(ngrid):
    return pltpu.CompilerParams(dimension_semantics=("arbitrary",) * ngrid, vmem_limit_bytes=VMEM_LIMIT)


def _bdot(a, b):
    return jnp.dot(a.astype(BF16), b.astype(BF16), preferred_element_type=F32)


def _bdot_nt(a, b):
    return lax.dot_general(a.astype(BF16), b.astype(BF16), (((1,), (1,)), ((), ())), preferred_element_type=F32)


def _bdot_tn(a, b):
    return lax.dot_general(a.astype(BF16), b.astype(BF16), (((0,), (0,)), ((), ())), preferred_element_type=F32)


def _split3(x):
    x1 = x.astype(BF16)
    r1 = x - x1.astype(F32)
    x2 = r1.astype(BF16)
    return x1, x2, (r1 - x2.astype(F32)).astype(BF16)


def _mask_dot(m, x):
    mb = m.astype(BF16)
    x1, x2, x3 = _split3(x)
    return (jnp.dot(mb, x1, preferred_element_type=F32) + jnp.dot(mb, x2, preferred_element_type=F32)
            + jnp.dot(mb, x3, preferred_element_type=F32))


def _dot_mask(x, m):
    mb = m.astype(BF16)
    x1, x2, x3 = _split3(x)
    return (jnp.dot(x1, mb, preferred_element_type=F32) + jnp.dot(x2, mb, preferred_element_type=F32)
            + jnp.dot(x3, mb, preferred_element_type=F32))


def _sigmoid(x):
    return 1.0 / (1.0 + jnp.exp(-x))


def _silu(x):
    return x * _sigmoid(x)


def _softplus(x):
    return jnp.maximum(x, 0.0) + jnp.log(1.0 + jnp.exp(-jnp.abs(x)))


def _layer_norm(y, g, b):
    yc = y - jnp.mean(y, -1, keepdims=True)
    var = jnp.mean(yc * yc, -1, keepdims=True)
    return yc * lax.rsqrt(var + LN_EPS) * g + b


def _iota(shape, dim):
    return lax.broadcasted_iota(jnp.int32, shape, dim)


def _tile_rows(x, n):
    return jnp.concatenate([x] * n, axis=0)


def _sum_row_blocks(x, n):
    r = x.shape[0] // n
    out = x[0:r]
    for i in range(1, n):
        out = out + x[i * r:(i + 1) * r]
    return out


def _unit_lower_inverse(x, lb):
    n = x.shape[0]
    eye = (_iota((n, n), 0) == _iota((n, n), 1)).astype(F32)
    p = eye + x
    y = x
    for _ in range(int(math.log2(lb)) - 1):
        y = _bdot(y, y)
        p = p + _bdot(p, y)
    return p


def _ffn_ln_kernel(h_ref, wg_ref, wu_ref, wd_ref, g_ref, b_ref, o_ref):
    x = h_ref[...]
    xb = x.astype(BF16)
    gate = jnp.dot(xb, wg_ref[...], preferred_element_type=F32)
    up = jnp.dot(xb, wu_ref[...], preferred_element_type=F32)
    act = (_silu(gate) * up).astype(BF16)
    f = jnp.dot(act, wd_ref[...], preferred_element_type=F32)
    o_ref[...] = _layer_norm(DEEPNORM_ALPHA * x + 0.5 * f, g_ref[...], b_ref[...])


def _resident(block_shape, index_map):
    return pl.BlockSpec(block_shape, index_map, pipeline_mode=pl.Buffered(1))


def ffn_ln(h, wg, wu, wd, ln_g, ln_b, l, j, k):
    t = h.shape[0]
    row = pl.BlockSpec((ROW_TILE, D_MODEL), lambda i: (i, 0))
    ln = pl.BlockSpec((None, 1, D_MODEL), lambda i: (3 * l + k, 0, 0))
    return pl.pallas_call(
        _ffn_ln_kernel,
        grid=(t // ROW_TILE,),
        in_specs=[row,
                  _resident((None, None, D_MODEL, D_FF), lambda i: (l, j, 0, 0)),
                  _resident((None, None, D_MODEL, D_FF), lambda i: (l, j, 0, 0)),
                  _resident((None, None, D_FF, D_MODEL), lambda i: (l, j, 0, 0)),
                  ln, ln],
        out_specs=row,
        out_shape=jax.ShapeDtypeStruct((t, D_MODEL), F32),
        compiler_params=_cparams(1),
        name="ffn_ln",
    )(h, wg, wu, wd, ln_g, ln_b)


def _proj_kernel(h_ref, w_ref, *o_refs):
    xb = h_ref[...].astype(BF16)
    off = 0
    for o_ref in o_refs:
        n = o_ref.shape[1]
        o_ref[...] = jnp.dot(xb, w_ref[:, off:off + n], preferred_element_type=F32)
        off += n


def project(h, w, i, widths):
    t = h.shape[0]
    ncols = w.shape[2]
    assert sum(widths) == ncols
    return pl.pallas_call(
        _proj_kernel,
        grid=(t // ROW_TILE,),
        in_specs=[pl.BlockSpec((ROW_TILE, D_MODEL), lambda r: (r, 0)),
                  _resident((None, D_MODEL, ncols), lambda r: (i, 0, 0))],
        out_specs=[pl.BlockSpec((ROW_TILE, n), lambda r: (r, 0)) for n in widths],
        out_shape=[jax.ShapeDtypeStruct((t, n), F32) for n in widths],
        compiler_params=_cparams(1),
        name="project",
    )(h, w)


def _out_ln_kernel(*refs):
    h_ref, o_refs, (w_ref, g_ref, b_ref, out_ref) = refs[0], refs[1:-4], refs[-4:]
    acc = DEEPNORM_ALPHA * h_ref[...]
    off = 0
    for o_ref in o_refs:
        n = o_ref.shape[1]
        acc = acc + jnp.dot(o_ref[...].astype(BF16), w_ref[off:off + n, :], preferred_element_type=F32)
        off += n
    out_ref[...] = _layer_norm(acc, g_ref[...], b_ref[...])


def out_ln(h, outs, w, i, ln_g, ln_b, l):
    t = h.shape[0]
    row = pl.BlockSpec((ROW_TILE, D_MODEL), lambda r: (r, 0))
    ln = pl.BlockSpec((None, 1, D_MODEL), lambda r: (3 * l + 1, 0, 0))
    return pl.pallas_call(
        _out_ln_kernel,
        grid=(t // ROW_TILE,),
        in_specs=[row] + [pl.BlockSpec((ROW_TILE, o.shape[1]), lambda r: (r, 0)) for o in outs]
        + [_resident((None, D_MODEL, D_MODEL), lambda r: (i, 0, 0)), ln, ln],
        out_specs=row,
        out_shape=jax.ShapeDtypeStruct((t, D_MODEL), F32),
        compiler_params=_cparams(1),
        name="out_ln",
    )(h, *outs, w, ln_g, ln_b)


def _row_info(cfg, shape, dim):
    r = _iota(shape, dim)
    return r >> cfg.shift, r & (cfg.lb - 1)


def _pick(cfg, mine, x):
    return x if cfg.nseq == 1 else jnp.where(mine, x, 0.0)


def _pair_masks(cfg, n):
    ri, ci = _iota((n, n), 0), _iota((n, n), 1)
    same = (ri >> cfg.shift) == (ci >> cfg.shift)
    ti, tj = ri & (cfg.lb - 1), ci & (cfg.lb - 1)
    return same & (tj <= ti), same & (tj < ti), same


def _ret_kernel(*refs, cfg):
    if cfg.has_state:
        zb_ref, s0_ref, g_ref, b_ref, o_ref, sout_ref, s_scr = refs
    else:
        zb_ref, g_ref, b_ref, o_ref, sout_ref, s_scr = refs
    c = pl.program_id(1)
    rows = cfg.rows

    @pl.when(c == 0)
    def _():
        s_scr[...] = s0_ref[...] if cfg.has_state else jnp.zeros(s_scr.shape, F32)

    seq_c, t_c = _row_info(cfg, (rows, 1), 0)
    live = t_c < cfg.valid
    incl, _, _ = _pair_masks(cfg, rows)
    ti = (_iota((rows, rows), 0) & (cfg.lb - 1))
    tj = (_iota((rows, rows), 1) & (cfg.lb - 1))
    steps_i = jnp.minimum(ti + 1, cfg.valid).astype(F32)
    steps_j = jnp.minimum(tj + 1, cfg.valid).astype(F32)
    steps_c = jnp.minimum(t_c + 1, cfg.valid).astype(F32)

    lane = _iota((rows, LANES), 1)
    half = DK_B // 2
    inv_freq = 1.0 / jnp.power(ROPE_THETA, (lane & (half - 1)).astype(F32) / (half - 1.0))
    pos = (cfg.pos0 + c * cfg.lb + t_c).astype(F32)
    ang = pos * inv_freq
    cos = jnp.cos(ang)
    sin = jnp.sin(ang)
    sin = jnp.where(lane < half, -sin, sin)

    def rot(x):
        return x * cos + pltpu.roll(x, half, 1) * sin

    for h in range(H_B):
        sl = slice(h * DK_B, (h + 1) * DK_B)
        q = rot(zb_ref[:, sl])
        k = rot(zb_ref[:, H_B * DK_B + h * DK_B:H_B * DK_B + (h + 1) * DK_B]) * DK_B ** -0.5
        v = zb_ref[:, 2 * H_B * DK_B + h * DV_B:2 * H_B * DK_B + (h + 1) * DV_B]
        gate = zb_ref[:, 2 * H_B * DK_B + C_B + h * DV_B:2 * H_B * DK_B + C_B + (h + 1) * DV_B]
        k = jnp.where(live, k, 0.0)
        v = jnp.where(live, v, 0.0)
        lam = math.log(1.0 - 2.0 ** (-5.0 - h))
        dmat = jnp.where(incl, jnp.exp(lam * (steps_i - steps_j)), 0.0)
        o = _bdot(_bdot_nt(q, k) * dmat, v)
        kz = k * jnp.exp(lam * (cfg.valid - steps_c))
        xi = jnp.exp(lam * steps_c)
        for s in range(cfg.nseq):
            mine = seq_c == s
            st = s_scr[s, h]
            o = o + _pick(cfg, mine, _bdot(q, st) * xi)
            s_scr[s, h] = math.exp(lam * cfg.valid) * st + _bdot_tn(_pick(cfg, mine, kz), v)
        oc = o - jnp.mean(o, -1, keepdims=True)
        var = jnp.mean(oc * oc, -1, keepdims=True)
        o = oc * lax.rsqrt(var + RET_GN_EPS) * g_ref[:, sl] + b_ref[:, sl]
        o_ref[:, sl] = o * _silu(gate)

    @pl.when(c == cfg.nc - 1)
    def _():
        sout_ref[...] = s_scr[...]


def retention(zb, s0, gn_g, gn_b, i, cfg):
    t = zb.shape[0]
    nb = t // (cfg.rows * cfg.nc)
    nc = cfg.nc
    state = pl.BlockSpec((cfg.nseq, H_B, DK_B, DV_B), lambda b, c: (b, 0, 0, 0))
    par = pl.BlockSpec((None, 1, C_B), lambda b, c: (i, 0, 0))
    in_specs = [pl.BlockSpec((cfg.rows, B_COLS), lambda b, c: (b * nc + c, 0))]
    args = [zb]
    if cfg.has_state:
        in_specs.append(pl.BlockSpec((None, cfg.nseq, H_B, DK_B, DV_B), lambda b, c: (i, b, 0, 0, 0)))
        args.append(s0)
    return pl.pallas_call(
        functools.partial(_ret_kernel, cfg=cfg),
        grid=(nb, nc),
        in_specs=in_specs + [par, par],
        out_specs=[pl.BlockSpec((cfg.rows, C_B), lambda b, c: (b * nc + c, 0)), state],
        out_shape=[jax.ShapeDtypeStruct((t, C_B), F32),
                   jax.ShapeDtypeStruct((nb * cfg.nseq, H_B, DK_B, DV_B), F32)],
        scratch_shapes=[pltpu.VMEM((cfg.nseq, H_B, DK_B, DV_B), F32)],
        compiler_params=_cparams(2),
        name="retention",
    )(*args, gn_g, gn_b)


def _gdn_kernel(*refs, cfg):
    if cfg.has_state:
        (qkv_ref, ba_ref, zg_ref, pv_ref, s0_ref, cw_ref, alog_ref, dtb_ref, ng_ref, selb_ref,
         o_ref, cvout_ref, sout_ref, s_scr, pv_scr) = refs
    else:
        (qkv_ref, ba_ref, zg_ref, cw_ref, alog_ref, dtb_ref, ng_ref, selb_ref,
         o_ref, cvout_ref, sout_ref, s_scr, pv_scr) = refs
    c = pl.program_id(1)
    rows, lb, valid, shift = cfg.rows, cfg.lb, cfg.valid, cfg.shift
    nprev = cfg.nseq * SUBLANES

    @pl.when(c == 0)
    def _():
        s_scr[...] = s0_ref[...] if cfg.has_state else jnp.zeros(s_scr.shape, F32)
        pv_scr[...] = pv_ref[...] if cfg.has_state else jnp.zeros(pv_scr.shape, F32)

    seq_c, t_c = _row_info(cfg, (rows, 1), 0)
    live = t_c < valid

    x = qkv_ref[...]
    pv = pv_scr[...]
    pr, pc = _iota((rows, nprev), 0), _iota((rows, nprev), 1)
    same_sp = (pr >> shift) == (pc >> 3)
    conv = cw_ref[CONV_W - 1:CONV_W, :] * x
    for s in range(1, CONV_W):
        sel = same_sp & ((pr & (lb - 1)) < s) & ((pc & 7) == (CONV_W - 1 - s) + (pr & (lb - 1)))
        xs = jnp.where(t_c >= s, pltpu.roll(x, s, 0), _mask_dot(sel, pv))
        conv = conv + cw_ref[CONV_W - 1 - s:CONV_W - s, :] * xs
    qr, qc = _iota((nprev, rows), 0), _iota((nprev, rows), 1)
    sel = (((qr >> 3) == (qc >> shift)) & ((qr & 7) < CONV_W - 1)
           & ((qc & (lb - 1)) == (valid - (CONV_W - 1)) + (qr & 7)))
    new_pv = _mask_dot(sel, x)
    pv_scr[...] = new_pv

    @pl.when(c == cfg.nc - 1)
    def _():
        cvout_ref[...] = new_pv

    qkv = _silu(conv)

    ba = ba_ref[...]
    beta_all = jnp.where(live, _sigmoid(ba[:, 0:LANES]), 0.0)
    g_all = jnp.where(live, -jnp.exp(alog_ref[...]) * _softplus(ba[:, LANES:2 * LANES] + dtb_ref[...]), 0.0)
    incl_r, _, same_r = _pair_masks(cfg, rows)
    gc_all = _mask_dot(incl_r, g_all)
    gl_all = _mask_dot(same_r, g_all)
    gx = _dot_mask(jnp.concatenate([beta_all, g_all, gc_all, gl_all], axis=0), selb_ref[...])
    bx, gxx, gcx, glx = (gx[j * rows:(j + 1) * rows] for j in range(4))
    n = GDN_GROUP * rows
    incl, strict, _ = _pair_masks(cfg, n)
    strict_f = strict.astype(F32)
    for gi in range(H_C // GDN_GROUP):
        heads = range(gi * GDN_GROUP, (gi + 1) * GDN_GROUP)

        def stack(arr, base):
            return jnp.concatenate([arr[:, base + DK_C * h:base + DK_C * (h + 1)] for h in heads], axis=0)

        def unit(a):
            return a * lax.rsqrt(jnp.maximum(jnp.sum(a * a, -1, keepdims=True), 1e-12))

        qs = unit(stack(qkv, 0)) * DK_C ** -0.5
        ks = unit(stack(qkv, H_C * DK_C))
        vs = stack(qkv, 2 * H_C * DK_C)
        bs, gs, gcs, gls = stack(bx, 0), stack(gxx, 0), stack(gcx, 0), stack(glx, 0)
        wide = lambda a: jnp.concatenate([a] * (n // LANES), axis=1)
        diff = _mask_dot(incl, wide(gs) * strict_f)
        dmat = jnp.where(incl, jnp.exp(diff), 0.0)
        kk = _bdot_nt(ks, ks)
        a = jnp.where(strict, wide(bs) * kk * dmat, 0.0)
        tinv = _unit_lower_inverse(-a, lb)
        egc = jnp.exp(gcs)
        u = _bdot(tinv, vs * bs)
        w = _bdot(tinv, ks * bs * egc)
        qk = _bdot_nt(qs, ks) * dmat
        qg = qs * egc
        kd = ks * jnp.exp(gls - gcs)
        egl = jnp.exp(gls)
        v_new, o_state = [], []
        for hh, h in enumerate(heads):
            rs = slice(hh * rows, (hh + 1) * rows)
            wq = jnp.concatenate([w[rs], qg[rs]], axis=0)
            ws = jnp.zeros((rows, DV_C), F32)
            os_ = jnp.zeros((rows, DV_C), F32)
            for s in range(cfg.nseq):
                res = _bdot(wq, s_scr[s, h])
                mine = seq_c == s
                ws = ws + _pick(cfg, mine, res[0:rows])
                os_ = os_ + _pick(cfg, mine, res[rows:2 * rows])
            v_new.append(u[rs] - ws)
            o_state.append(os_)
        v_new_s = jnp.concatenate(v_new, axis=0)
        o_s = jnp.concatenate(o_state, axis=0) + _bdot(qk, v_new_s)
        for hh, h in enumerate(heads):
            rs = slice(hh * rows, (hh + 1) * rows)
            for s in range(cfg.nseq):
                mine = seq_c == s
                r0 = hh * rows + s * lb
                s_scr[s, h] = (egl[r0:r0 + 1, :] * s_scr[s, h]
                               + _bdot_tn(_pick(cfg, mine, kd[rs]), v_new[hh]))
            o = o_s[rs]
            o = o * lax.rsqrt(jnp.mean(o * o, -1, keepdims=True) + 1e-6) * ng_ref[...]
            o_ref[:, h * DV_C:(h + 1) * DV_C] = o * _silu(zg_ref[:, h * DV_C:(h + 1) * DV_C])

    @pl.when(c == cfg.nc - 1)
    def _():
        sout_ref[...] = s_scr[...]


def gdn(qkv, ba, zg, conv_prev, s0, conv_w, a_log, dt_bias, norm_g, selb, i, cfg):
    t = qkv.shape[0]
    nb = t // (cfg.rows * cfg.nc)
    nc = cfg.nc
    nprev = cfg.nseq * SUBLANES
    rowspec = lambda n: pl.BlockSpec((cfg.rows, n), lambda b, c: (b * nc + c, 0))
    prev = pl.BlockSpec((nprev, N_QKV_C), lambda b, c: (b, 0))
    prev_in = pl.BlockSpec((None, nprev, N_QKV_C), lambda b, c: (i, b, 0))
    state = pl.BlockSpec((cfg.nseq, H_C, DK_C, DV_C), lambda b, c: (b, 0, 0, 0))
    state_in = pl.BlockSpec((None, cfg.nseq, H_C, DK_C, DV_C), lambda b, c: (i, b, 0, 0, 0))
    par = lambda n: pl.BlockSpec((None, 1, n), lambda b, c: (i, 0, 0))
    in_specs = [rowspec(N_QKV_C), rowspec(2 * LANES), rowspec(C_C)]
    args = [qkv, ba, zg]
    if cfg.has_state:
        in_specs += [prev_in, state_in]
        args += [conv_prev, s0]
    in_specs += [pl.BlockSpec((None, CONV_W, N_QKV_C), lambda b, c: (i, 0, 0)), par(LANES), par(LANES), par(DV_C),
                 pl.BlockSpec(selb.shape, lambda b, c: (0, 0))]
    args += [conv_w, a_log, dt_bias, norm_g, selb]
    return pl.pallas_call(
        functools.partial(_gdn_kernel, cfg=cfg),
        grid=(nb, nc),
        in_specs=in_specs,
        out_specs=[rowspec(C_C), prev, state],
        out_shape=[jax.ShapeDtypeStruct((t, C_C), F32),
                   jax.ShapeDtypeStruct((nb * nprev, N_QKV_C), F32),
                   jax.ShapeDtypeStruct((nb * cfg.nseq, H_C, DK_C, DV_C), F32)],
        scratch_shapes=[pltpu.VMEM((cfg.nseq, H_C, DK_C, DV_C), F32), pltpu.VMEM((nprev, N_QKV_C), F32)],
        compiler_params=_cparams(2),
        name="gdn",
    )(*args)


RWKV_VEC_ROWS = 8
RWKV_GL = RWKV_GROUP * N_A


def _rwkv_kernel(*refs, cfg):
    if cfg.has_state:
        (za_ref, pv_ref, s0_ref, mu_ref, vec_ref, lw_ref, ones_ref,
         o_ref, shout_ref, sout_ref, s_scr, pv_scr) = refs
    else:
        (za_ref, mu_ref, vec_ref, lw_ref, ones_ref, o_ref, shout_ref, sout_ref, s_scr, pv_scr) = refs
    c = pl.program_id(1)
    rows, lb, valid, shift = cfg.rows, cfg.lb, cfg.valid, cfg.shift
    ngroups = H_A // RWKV_GROUP
    n = RWKV_GROUP * rows
    head_shift = int(math.log2(N_A))
    row_shift = int(math.log2(rows))
    assert 1 << row_shift == rows and cfg.nseq <= SUBLANES

    bi, bj = _iota((RWKV_GL, RWKV_GL), 0), _iota((RWKV_GL, RWKV_GL), 1)
    blockdiag = (bi >> head_shift) == (bj >> head_shift)

    @pl.when(c == 0)
    def _():
        pv_scr[...] = pv_ref[...] if cfg.has_state else jnp.zeros(pv_scr.shape, F32)
        for s in range(cfg.nseq):
            for g in range(ngroups):
                if cfg.has_state:
                    sc = s0_ref[s, :, g * RWKV_GL:(g + 1) * RWKV_GL]
                    s_scr[s * ngroups + g] = jnp.where(blockdiag, _tile_rows(sc, RWKV_GROUP), 0.0)
                else:
                    s_scr[s * ngroups + g] = jnp.zeros((RWKV_GL, RWKV_GL), F32)

    seq_c, t_c = _row_info(cfg, (rows, 1), 0)
    live = t_c < valid

    za = za_ref[...]
    pr, pc = _iota((rows, SUBLANES), 0), _iota((rows, SUBLANES), 1)
    first = ((pr >> shift) == pc) & ((pr & (lb - 1)) == 0)
    za_prev = jnp.where(t_c >= 1, pltpu.roll(za, 1, 0), _mask_dot(first, pv_scr[...]))
    qr, qc = _iota((SUBLANES, rows), 0), _iota((SUBLANES, rows), 1)
    last = (qr == (qc >> shift)) & ((qc & (lb - 1)) == valid - 1)
    new_pv = _mask_dot(last, za)
    pv_scr[...] = new_pv

    @pl.when(c == cfg.nc - 1)
    def _():
        shout_ref[...] = new_pv

    zs = za + (za_prev - za) * mu_ref[...]
    r = zs[:, 0:C_A]
    k = zs[:, C_A:2 * C_A]
    v = zs[:, 2 * C_A:3 * C_A]
    lo = zs[:, 3 * C_A:A_SHIFT_COLS]
    ll = _iota((rows, LORA_COLS), 1)
    lo = jnp.where(ll < W_LORA, jnp.tanh(lo), jnp.where(ll < W_LORA + A_LORA, lo, _sigmoid(lo)))
    w0, a0, k_k, k_a, r_k, gn_g, gn_b = (vec_ref[j:j + 1, :] for j in range(7))
    w_log = -_softplus(-(w0 + _bdot(lo, lw_ref[0]))) - 0.5
    logw = -jnp.exp(w_log)
    a = _sigmoid(a0 + _bdot(lo, lw_ref[1]))
    gate = _bdot(lo, lw_ref[2])
    ones_blk = ones_ref[...]
    kk = k * k_k
    kk = kk * lax.rsqrt(jnp.maximum(_bdot(kk * kk, ones_blk), 1e-12))
    k = k * (1.0 + (a - 1.0) * k_a)
    bonus = _bdot(r * k * r_k, ones_blk) * v
    logw = jnp.where(live, logw, 0.0)
    kk = jnp.where(live, kk, 0.0)
    k = jnp.where(live, k, 0.0)
    v = jnp.where(live, v, 0.0)
    ka = kk * a

    incl_r, _, same_r = _pair_masks(cfg, rows)
    tj = _iota((rows, rows), 1) & (lb - 1)
    sums = _mask_dot(jnp.concatenate([incl_r.astype(F32), (same_r & (tj < lb // 2)).astype(F32), same_r.astype(F32)],
                                     axis=0), logw)
    cl, cm, ct = sums[0:rows], sums[rows:2 * rows], sums[2 * rows:3 * rows]
    clp = cl - logw
    b0 = kk * jnp.exp(clp)
    r0 = r * jnp.exp(cl)
    bc = kk * jnp.exp(clp - cm)
    rc = r * jnp.exp(cl - cm)
    ac = -ka * jnp.exp(cm - cl)
    kc = k * jnp.exp(cm - cl)
    ah = -ka * jnp.exp(ct - cl)
    kh = k * jnp.exp(ct - cl)
    wc = jnp.exp(ct)

    hi, hl = _iota((n, RWKV_GL), 0), _iota((n, RWKV_GL), 1)
    headmask = (hi >> row_shift) == (hl >> head_shift)
    incl, strict, _ = _pair_masks(cfg, n)
    spread = lambda x: jnp.where(headmask, _tile_rows(x, RWKV_GROUP), 0.0)
    outs = []
    for g in range(ngroups):
        sl = slice(g * RWKV_GL, (g + 1) * RWKV_GL)
        gram = _bdot_nt(jnp.concatenate([spread(bc[:, sl]), spread(rc[:, sl])], axis=0),
                        jnp.concatenate([_tile_rows(ac[:, sl], RWKV_GROUP), _tile_rows(kc[:, sl], RWKV_GROUP)], axis=0))
        x1 = jnp.where(strict, gram[0:n, 0:n], 0.0)
        x2 = jnp.where(strict, gram[0:n, n:2 * n], 0.0)
        x3 = jnp.where(incl, gram[n:2 * n, 0:n], 0.0)
        x4 = jnp.where(incl, gram[n:2 * n, n:2 * n], 0.0)
        tinv = _unit_lower_inverse(x1, lb)
        br = jnp.concatenate([b0[:, sl], r0[:, sl]], axis=0)
        u0 = jnp.zeros((rows, RWKV_GL), F32)
        o0 = jnp.zeros((rows, RWKV_GL), F32)
        for s in range(cfg.nseq):
            res = _bdot_nt(br, s_scr[s * ngroups + g])
            mine = seq_c == s
            u0 = u0 + _pick(cfg, mine, res[0:rows])
            o0 = o0 + _pick(cfg, mine, res[rows:2 * rows])
        vh = spread(v[:, sl])
        uh = _bdot(tinv, spread(u0) + _bdot(x2, vh))
        oh = _bdot(x3, uh) + _bdot(x4, vh)
        outs.append(o0 + _sum_row_blocks(oh, RWKV_GROUP))
        uv = jnp.concatenate([_sum_row_blocks(uh, RWKV_GROUP), v[:, sl]], axis=0)
        akh = jnp.concatenate([ah[:, sl], kh[:, sl]], axis=0)
        for s in range(cfg.nseq):
            mine2 = jnp.concatenate([seq_c == s] * 2, axis=0)
            upd = _bdot_tn(_pick(cfg, mine2, uv), akh)
            s_scr[s * ngroups + g] = (s_scr[s * ngroups + g] * wc[s * lb:s * lb + 1, sl]
                                      + jnp.where(blockdiag, upd, 0.0))
    o = jnp.concatenate(outs, axis=1)
    mean = _bdot(o, ones_blk) * (1.0 / N_A)
    oc = o - mean
    var = _bdot(oc * oc, ones_blk) * (1.0 / N_A)
    o = oc * lax.rsqrt(var + RWKV_GN_EPS) * gn_g + gn_b
    o_ref[...] = (o + bonus) * gate

    @pl.when(c == cfg.nc - 1)
    def _():
        for s in range(cfg.nseq):
            for g in range(ngroups):
                sout_ref[s, :, g * RWKV_GL:(g + 1) * RWKV_GL] = _sum_row_blocks(s_scr[s * ngroups + g], RWKV_GROUP)


def rwkv(za, shift_prev, s0, mu, vecs, lora_w, ones_blk, i, cfg):
    t = za.shape[0]
    nb = t // (cfg.rows * cfg.nc)
    nc = cfg.nc
    ngroups = H_A // RWKV_GROUP
    state = pl.BlockSpec((cfg.nseq, N_A, C_A), lambda b, c: (b, 0, 0))
    hist = pl.BlockSpec((SUBLANES, A_SHIFT_COLS), lambda b, c: (b, 0))
    in_specs = [pl.BlockSpec((cfg.rows, A_SHIFT_COLS), lambda b, c: (b * nc + c, 0))]
    args = [za]
    if cfg.has_state:
        assert cfg.nseq == SUBLANES
        in_specs += [pl.BlockSpec((None, SUBLANES, A_SHIFT_COLS), lambda b, c: (i, b, 0)),
                     pl.BlockSpec((None, cfg.nseq, N_A, C_A), lambda b, c: (i, b, 0, 0))]
        args += [shift_prev, s0]
    in_specs += [pl.BlockSpec((None, 1, A_SHIFT_COLS), lambda b, c: (i, 0, 0)),
                 pl.BlockSpec((None, RWKV_VEC_ROWS, C_A), lambda b, c: (i, 0, 0)),
                 pl.BlockSpec((None, 3, LORA_COLS, C_A), lambda b, c: (i, 0, 0, 0)),
                 pl.BlockSpec((C_A, C_A), lambda b, c: (0, 0))]
    args += [mu, vecs, lora_w, ones_blk]
    return pl.pallas_call(
        functools.partial(_rwkv_kernel, cfg=cfg),
        grid=(nb, nc),
        in_specs=in_specs,
        out_specs=[pl.BlockSpec((cfg.rows, C_A), lambda b, c: (b * nc + c, 0)), hist, state],
        out_shape=[jax.ShapeDtypeStruct((t, C_A), F32),
                   jax.ShapeDtypeStruct((nb * SUBLANES, A_SHIFT_COLS), F32),
                   jax.ShapeDtypeStruct((nb * cfg.nseq, N_A, C_A), F32)],
        scratch_shapes=[pltpu.VMEM((cfg.nseq * ngroups, RWKV_GL, RWKV_GL), F32),
                        pltpu.VMEM((SUBLANES, A_SHIFT_COLS), F32)],
        compiler_params=_cparams(2),
        name="rwkv",
    )(*args)


def _trunk(x, cfg_rwkv, cfg_ret, cfg_gdn, st, p):
    has = st is not None
    h = x
    new = dict(shift=[], rwkv=[], ret=[], conv=[], gdn=[])
    for l in range(DEPTH):
        i = l // 2
        h = ffn_ln(h, p["wg"], p["wu"], p["wd"], p["ln_g"], p["ln_b"], l, 0, 0)
        if l % 2 == 0:
            za, zb = project(h, p["ab_in"], i, (A_SHIFT_COLS, B_COLS))
            o_a, sh, s_rw = rwkv(za, st["shift"] if has else None, st["rwkv"] if has else None,
                                 p["mu"], p["rwkv_vecs"], p["lora_w"], p["ones_blk"], i, cfg_rwkv)
            o_b, s_rt = retention(zb, st["ret"] if has else None, p["ret_g"], p["ret_b"], i, cfg_ret)
            h = out_ln(h, [o_a, o_b], p["ab_out"], i, p["ln_g"], p["ln_b"], l)
            new["shift"].append(sh)
            new["rwkv"].append(s_rw)
            new["ret"].append(s_rt)
        else:
            qkv, ba, zg = project(h, p["gdn_in"], i, (N_QKV_C, 2 * LANES, C_C))
            o, cv, s_gd = gdn(qkv, ba, zg, st["conv"] if has else None, st["gdn"] if has else None,
                              p["conv_w"], p["a_log"], p["dt_bias"], p["norm_g"], p["selb"], i, cfg_gdn)
            h = out_ln(h, [o], p["gdn_out"], i, p["ln_g"], p["ln_b"], l)
            new["conv"].append(cv)
            new["gdn"].append(s_gd)
        h = ffn_ln(h, p["wg"], p["wu"], p["wd"], p["ln_g"], p["ln_b"], l, 1, 2)
    return h, {k: jnp.stack(v) for k, v in new.items()}


def _lane_pad(x, n):
    return jnp.pad(x, [(0, 0)] * (x.ndim - 1) + [(0, n - x.shape[-1])])


def kernel(x_prompt, x_sample, state_rwkv_shift, state_rwkv, state_ret, state_gdn_conv, state_gdn, ln_g, ln_b, ffn_w_gate, ffn_w_up, ffn_w_down, ab_w_in, ab_w_out, rwkv_mu, rwkv_w0, rwkv_w2, rwkv_a0, rwkv_a2, rwkv_g2, rwkv_k_k, rwkv_k_a, rwkv_r_k, rwkv_gn_g, rwkv_gn_b, ret_gn_g, ret_gn_b, gdn_w_in, gdn_conv_w, gdn_A_log, gdn_dt_bias, gdn_norm_g, gdn_w_out):
    nb, seq = x_prompt.shape[0], x_prompt.shape[1]
    ndec, dec_seq = x_sample.shape[0], x_sample.shape[1]
    nl = N_AB_LAYERS

    lora_w = jnp.zeros((nl, 3, LORA_COLS, C_A), F32)
    lora_w = (lora_w.at[:, 0, 0:W_LORA].set(rwkv_w2).at[:, 1, W_LORA:W_LORA + A_LORA].set(rwkv_a2)
              .at[:, 2, W_LORA + A_LORA:].set(rwkv_g2))
    gdn_in = jnp.concatenate([gdn_w_in[:, :, :N_QKV_C],
                              _lane_pad(gdn_w_in[:, :, N_QKV_C:N_QKV_C + H_C], LANES),
                              _lane_pad(gdn_w_in[:, :, N_QKV_C + H_C:N_QKV_C + 2 * H_C], LANES),
                              gdn_w_in[:, :, N_QKV_C + 2 * H_C:]], axis=2)
    lane_head = np.arange(C_A) // N_A
    sel = np.zeros((LANES, H_C * LANES), np.float32)
    for hd in range(H_C):
        sel[hd, hd * LANES:(hd + 1) * LANES] = 1.0
    p = dict(
        wg=ffn_w_gate.astype(BF16), wu=ffn_w_up.astype(BF16), wd=ffn_w_down.astype(BF16),
        ln_g=ln_g.reshape(DEPTH * 3, 1, D_MODEL), ln_b=ln_b.reshape(DEPTH * 3, 1, D_MODEL),
        ab_in=ab_w_in.astype(BF16), ab_out=ab_w_out.astype(BF16),
        gdn_in=gdn_in.astype(BF16), gdn_out=gdn_w_out.astype(BF16),
        mu=rwkv_mu.reshape(nl, 1, A_SHIFT_COLS),
        rwkv_vecs=jnp.stack([rwkv_w0, rwkv_a0, rwkv_k_k, rwkv_k_a, rwkv_r_k.reshape(nl, C_A), rwkv_gn_g, rwkv_gn_b,
                             jnp.zeros_like(rwkv_w0)], axis=1),
        lora_w=lora_w.astype(BF16),
        ones_blk=jnp.asarray((lane_head[:, None] == lane_head[None, :]).astype(np.float32)),
        ret_g=ret_gn_g.reshape(nl, 1, C_B), ret_b=ret_gn_b.reshape(nl, 1, C_B),
        conv_w=gdn_conv_w, a_log=_lane_pad(gdn_A_log, LANES)[:, None, :], dt_bias=_lane_pad(gdn_dt_bias, LANES)[:, None, :],
        norm_g=gdn_norm_g[:, None, :], selb=jnp.asarray(sel),
    )

    hp, sp = _trunk(x_prompt.reshape(nb * seq, D_MODEL),
                    SeqCfg(1, 64, 64, seq // 64, 0, False), SeqCfg(1, 128, 128, seq // 128, 0, False),
                    SeqCfg(1, 64, 64, seq // 64, 0, False), None, p)
    cfg = SeqCfg(SEQ_PER_STEP, SAMPLE_PAD, dec_seq, 1, PAST_LEN, True)
    xs = jnp.pad(x_sample, ((0, 0), (0, SAMPLE_PAD - dec_seq), (0, 0))).reshape(ndec * SAMPLE_PAD, D_MODEL)
    st = dict(shift=state_rwkv_shift,
              rwkv=state_rwkv.transpose(0, 1, 3, 2, 4).reshape(nl, ndec, N_A, C_A),
              ret=state_ret,
              conv=jnp.pad(state_gdn_conv, ((0, 0), (0, 0), (0, SUBLANES - (CONV_W - 1)), (0, 0))
                           ).reshape(N_C_LAYERS, ndec * SUBLANES, N_QKV_C),
              gdn=state_gdn)
    hs, ss = _trunk(xs, cfg, cfg, cfg, st, p)

    def unpack(s, n, strided_shift):
        shift = s["shift"][:, ::SUBLANES] if strided_shift else s["shift"]
        return (shift,
                s["rwkv"].reshape(nl, n, N_A, H_A, N_A).transpose(0, 1, 3, 2, 4),
                s["ret"],
                s["conv"].reshape(N_C_LAYERS, n, SUBLANES, N_QKV_C)[:, :, :CONV_W - 1],
                s["gdn"])

    y_prompt = hp.reshape(nb, seq, D_MODEL)
    y_sample = hs.reshape(ndec, SAMPLE_PAD, D_MODEL)[:, :dec_seq]
    return (y_prompt, y_sample) + unpack(sp, nb, True) + unpack(ss, ndec, False)
```

```python
import functools
import math

import numpy as np
import jax
import jax.numpy as jnp
from jax import lax
from jax.experimental import pallas as pl
from jax.experimental.pallas import tpu as pltpu

F32 = jnp.float32
BF16 = jnp.bfloat16

D_MODEL = 1024
DEPTH = 4
PAST_LEN = 16384
N_AB_LAYERS = (DEPTH + 1) // 2
N_C_LAYERS = DEPTH // 2
H_A, N_A = 8, 64
C_A = H_A * N_A
W_LORA, A_LORA, G_LORA = 64, 64, 128
LORA_COLS = W_LORA + A_LORA + G_LORA
A_SHIFT_COLS = 3 * C_A + LORA_COLS
RWKV_GN_EPS = 64e-5
H_B, DK_B, DV_B = 4, 128, 128
C_B = H_B * DV_B
ROPE_THETA = 10000.0
RET_GN_EPS = 1e-5
B_COLS = 2 * H_B * DK_B + 2 * C_B
AB_COLS = A_SHIFT_COLS + B_COLS
H_C, DK_C, DV_C = 8, 128, 128
C_C = H_C * DV_C
CONV_W = 4
N_QKV_C = 2 * H_C * DK_C + C_C
D_FF = 2816
LN_EPS = 1e-5
DEEPNORM_ALPHA = (2 * DEPTH) ** 0.25

V7X_VMEM_BYTES = 64 * 1024 * 1024
LANES = 128
SUBLANES = 8
VMEM_LIMIT = V7X_VMEM_BYTES * 7 // 8

SAMPLE_PAD = SUBLANES
SEQ_PER_UNIT = 8
UNITS = 2
ROW_TILE = 512
RWKV_GROUP = 4
GDN_GROUP = 4


class SeqCfg:
    def __init__(self, nseq, lb, valid, nc, pos0, has_state):
        self.nseq, self.lb, self.valid, self.nc, self.pos0, self.has_state = nseq, lb, valid, nc, pos0, has_state
        self.rows = nseq * lb
        self.shift = int(math.log2(lb))
        self.hist = nseq * SUBLANES
        assert 1 << self.shift == lb and lb >= SUBLANES and CONV_W - 1 <= valid <= lb


def _cparams(ngrid):
    return pltpu.CompilerParams(dimension_semantics=("arbitrary",) * ngrid, vmem_limit_bytes=VMEM_LIMIT)


def _bdot(a, b):
    return jnp.dot(a.astype(BF16), b.astype(BF16), preferred_element_type=F32)


def _bdot_nt(a, b):
    return lax.dot_general(a.astype(BF16), b.astype(BF16), (((1,), (1,)), ((), ())), preferred_element_type=F32)


def _bdot_tn(a, b):
    return lax.dot_general(a.astype(BF16), b.astype(BF16), (((0,), (0,)), ((), ())), preferred_element_type=F32)


def _split3(x):
    x1 = x.astype(BF16)
    r1 = x - x1.astype(F32)
    x2 = r1.astype(BF16)
    return x1, x2, (r1 - x2.astype(F32)).astype(BF16)


def _mask_dot(m, x):
    mb = m.astype(BF16)
    x1, x2, x3 = _split3(x)
    return (jnp.dot(mb, x1, preferred_element_type=F32) + jnp.dot(mb, x2, preferred_element_type=F32)
            + jnp.dot(mb, x3, preferred_element_type=F32))


def _dot_mask(x, m):
    mb = m.astype(BF16)
    x1, x2, x3 = _split3(x)
    return (jnp.dot(x1, mb, preferred_element_type=F32) + jnp.dot(x2, mb, preferred_element_type=F32)
            + jnp.dot(x3, mb, preferred_element_type=F32))


def _sigmoid(x):
    return 1.0 / (1.0 + jnp.exp(-x))


def _silu(x):
    return x * _sigmoid(x)


def _softplus(x):
    return jnp.maximum(x, 0.0) + jnp.log(1.0 + jnp.exp(-jnp.abs(x)))


def _layer_norm(y, g, b):
    yc = y - jnp.mean(y, -1, keepdims=True)
    var = jnp.mean(yc * yc, -1, keepdims=True)
    return yc * lax.rsqrt(var + LN_EPS) * g + b


def _iota(shape, dim):
    return lax.broadcasted_iota(jnp.int32, shape, dim)


def _tile_rows(x, n):
    return jnp.concatenate([x] * n, axis=0)


def _sum_row_blocks(x, n):
    r = x.shape[0] // n
    out = x[0:r]
    for i in range(1, n):
        out = out + x[i * r:(i + 1) * r]
    return out


def _unit_lower_inverse(x, lb):
    n = x.shape[0]
    eye = (_iota((n, n), 0) == _iota((n, n), 1)).astype(F32)
    p = eye + x
    y = x
    for _ in range(int(math.log2(lb)) - 1):
        y = _bdot(y, y)
        yield
        p = p + _bdot(p, y)
        yield
    return p


def _interleave(gens):
    gens = list(gens)
    while gens:
        for g in list(gens):
            try:
                next(g)
            except StopIteration:
                gens.remove(g)


def _ffn_ln_kernel(h_ref, wg_ref, wu_ref, wd_ref, g_ref, b_ref, o_ref):
    x = h_ref[...]
    xb = x.astype(BF16)
    gate = jnp.dot(xb, wg_ref[...], preferred_element_type=F32)
    up = jnp.dot(xb, wu_ref[...], preferred_element_type=F32)
    act = (_silu(gate) * up).astype(BF16)
    f = jnp.dot(act, wd_ref[...], preferred_element_type=F32)
    o_ref[...] = _layer_norm(DEEPNORM_ALPHA * x + 0.5 * f, g_ref[...], b_ref[...])


def _resident(block_shape, index_map):
    return pl.BlockSpec(block_shape, index_map, pipeline_mode=pl.Buffered(1))


def ffn_ln(h, wg, wu, wd, ln_g, ln_b, l, j, k):
    t = h.shape[0]
    row = pl.BlockSpec((ROW_TILE, D_MODEL), lambda i: (i, 0))
    ln = pl.BlockSpec((None, 1, D_MODEL), lambda i: (3 * l + k, 0, 0))
    return pl.pallas_call(
        _ffn_ln_kernel,
        grid=(t // ROW_TILE,),
        in_specs=[row,
                  _resident((None, None, D_MODEL, D_FF), lambda i: (l, j, 0, 0)),
                  _resident((None, None, D_MODEL, D_FF), lambda i: (l, j, 0, 0)),
                  _resident((None, None, D_FF, D_MODEL), lambda i: (l, j, 0, 0)),
                  ln, ln],
        out_specs=row,
        out_shape=jax.ShapeDtypeStruct((t, D_MODEL), F32),
        compiler_params=_cparams(1),
        name="ffn_ln",
    )(h, wg, wu, wd, ln_g, ln_b)


def _proj_kernel(h_ref, w_ref, *o_refs):
    xb = h_ref[...].astype(BF16)
    off = 0
    for o_ref in o_refs:
        n = o_ref.shape[1]
        o_ref[...] = jnp.dot(xb, w_ref[:, off:off + n], preferred_element_type=F32)
        off += n


def project(h, w, i, widths):
    t = h.shape[0]
    ncols = w.shape[2]
    assert sum(widths) == ncols
    return pl.pallas_call(
        _proj_kernel,
        grid=(t // ROW_TILE,),
        in_specs=[pl.BlockSpec((ROW_TILE, D_MODEL), lambda r: (r, 0)),
                  _resident((None, D_MODEL, ncols), lambda r: (i, 0, 0))],
        out_specs=[pl.BlockSpec((ROW_TILE, n), lambda r: (r, 0)) for n in widths],
        out_shape=[jax.ShapeDtypeStruct((t, n), F32) for n in widths],
        compiler_params=_cparams(1),
        name="project",
    )(h, w)


def _out_ln_kernel(*refs):
    h_ref, o_refs, (w_ref, g_ref, b_ref, out_ref) = refs[0], refs[1:-4], refs[-4:]
    acc = DEEPNORM_ALPHA * h_ref[...]
    off = 0
    for o_ref in o_refs:
        n = o_ref.shape[1]
        acc = acc + jnp.dot(o_ref[...].astype(BF16), w_ref[off:off + n, :], preferred_element_type=F32)
        off += n
    out_ref[...] = _layer_norm(acc, g_ref[...], b_ref[...])


def out_ln(h, outs, w, i, ln_g, ln_b, l):
    t = h.shape[0]
    row = pl.BlockSpec((ROW_TILE, D_MODEL), lambda r: (r, 0))
    ln = pl.BlockSpec((None, 1, D_MODEL), lambda r: (3 * l + 1, 0, 0))
    return pl.pallas_call(
        _out_ln_kernel,
        grid=(t // ROW_TILE,),
        in_specs=[row] + [pl.BlockSpec((ROW_TILE, o.shape[1]), lambda r: (r, 0)) for o in outs]
        + [_resident((None, D_MODEL, D_MODEL), lambda r: (i, 0, 0)), ln, ln],
        out_specs=row,
        out_shape=jax.ShapeDtypeStruct((t, D_MODEL), F32),
        compiler_params=_cparams(1),
        name="out_ln",
    )(h, *outs, w, ln_g, ln_b)


def _row_info(cfg, shape, dim):
    r = _iota(shape, dim)
    return r >> cfg.shift, r & (cfg.lb - 1)


def _pick(cfg, mine, x):
    return x if cfg.nseq == 1 else jnp.where(mine, x, 0.0)


def _pair_masks(cfg, n):
    ri, ci = _iota((n, n), 0), _iota((n, n), 1)
    same = (ri >> cfg.shift) == (ci >> cfg.shift)
    ti, tj = ri & (cfg.lb - 1), ci & (cfg.lb - 1)
    return same & (tj <= ti), same & (tj < ti), same


def _shifted(cfg, x, hist, t_c, s):
    hexp = hist if cfg.hist == cfg.rows else jnp.concatenate([hist, x[cfg.hist:]], axis=0)
    return jnp.where(t_c >= s, pltpu.roll(x, s, 0), pltpu.roll(hexp, cfg.rows - SUBLANES + s, 0))


def _new_history(cfg, x):
    rolled = x if cfg.lb == cfg.valid else pltpu.roll(x, cfg.lb - cfg.valid, 0)
    return jnp.concatenate([rolled[(j + 1) * cfg.lb - SUBLANES:(j + 1) * cfg.lb] for j in range(cfg.nseq)], axis=0)


def _seq_call(kernel, cfg, name, rows_in, hist_in, state_in, params, rows_out, hist_out, state_out, scratch, i):
    t = rows_in[0].shape[0]
    nbt = t // (cfg.rows * cfg.nc)
    assert nbt % UNITS == 0
    grid = (nbt // UNITS, cfg.nc)

    def zeros(n):
        return (0,) * n

    row_spec = lambda n: pl.BlockSpec((UNITS, None, cfg.rows, n), lambda b, c: (b, c, 0, 0))
    args = [a.reshape(nbt, cfg.nc, cfg.rows, a.shape[1]) for a in rows_in]
    in_specs = [row_spec(a.shape[1]) for a in rows_in]
    if cfg.has_state:
        for a in hist_in:
            args.append(a)
            in_specs.append(pl.BlockSpec((None, UNITS * cfg.hist, a.shape[2]), lambda b, c: (i, b, 0)))
        for a in state_in:
            nd = a.ndim - 2
            args.append(a)
            in_specs.append(pl.BlockSpec((None, UNITS * cfg.nseq) + a.shape[2:],
                                         lambda b, c, nd=nd: (i, b) + zeros(nd)))
    for a, bs, im in params:
        args.append(a)
        in_specs.append(pl.BlockSpec(bs, im))
    out_specs = [row_spec(n) for n in rows_out]
    out_shape = [jax.ShapeDtypeStruct((nbt, cfg.nc, cfg.rows, n), F32) for n in rows_out]
    for n in hist_out:
        out_specs.append(pl.BlockSpec((UNITS * cfg.hist, n), lambda b, c: (b, 0)))
        out_shape.append(jax.ShapeDtypeStruct((nbt * cfg.hist, n), F32))
    for shp in state_out:
        out_specs.append(pl.BlockSpec((UNITS * cfg.nseq,) + shp, lambda b, c, nd=len(shp): (b,) + zeros(nd)))
        out_shape.append(jax.ShapeDtypeStruct((nbt * cfg.nseq,) + shp, F32))
    outs = pl.pallas_call(
        functools.partial(kernel, cfg=cfg),
        grid=grid, in_specs=in_specs, out_specs=out_specs, out_shape=out_shape,
        scratch_shapes=scratch, compiler_params=_cparams(2), name=name,
    )(*args)
    return [o.reshape(t, o.shape[-1]) for o in outs[:len(rows_out)]] + list(outs[len(rows_out):])


def _ret_kernel(*refs, cfg):
    if cfg.has_state:
        zb_ref, s0_ref, g_ref, b_ref, o_ref, sout_ref, s_scr = refs
    else:
        zb_ref, g_ref, b_ref, o_ref, sout_ref, s_scr = refs
    c = pl.program_id(1)
    rows = cfg.rows

    @pl.when(c == 0)
    def _():
        s_scr[...] = s0_ref[...] if cfg.has_state else jnp.zeros(s_scr.shape, F32)

    seq_c, t_c = _row_info(cfg, (rows, 1), 0)
    live = t_c < cfg.valid
    incl, _, _ = _pair_masks(cfg, rows)
    ti = (_iota((rows, rows), 0) & (cfg.lb - 1))
    tj = (_iota((rows, rows), 1) & (cfg.lb - 1))
    steps_i = jnp.minimum(ti + 1, cfg.valid).astype(F32)
    steps_j = jnp.minimum(tj + 1, cfg.valid).astype(F32)
    steps_c = jnp.minimum(t_c + 1, cfg.valid).astype(F32)

    lane = _iota((rows, LANES), 1)
    half = DK_B // 2
    inv_freq = 1.0 / jnp.power(ROPE_THETA, (lane & (half - 1)).astype(F32) / (half - 1.0))
    pos = (cfg.pos0 + c * cfg.lb + t_c).astype(F32)
    ang = pos * inv_freq
    cos = jnp.cos(ang)
    sin = jnp.sin(ang)
    sin = jnp.where(lane < half, -sin, sin)

    def rot(x):
        return x * cos + pltpu.roll(x, half, 1) * sin

    def head(u, h):
        sl = slice(h * DK_B, (h + 1) * DK_B)
        q = rot(zb_ref[u, :, sl])
        k = rot(zb_ref[u, :, H_B * DK_B + h * DK_B:H_B * DK_B + (h + 1) * DK_B]) * DK_B ** -0.5
        v = zb_ref[u, :, 2 * H_B * DK_B + h * DV_B:2 * H_B * DK_B + (h + 1) * DV_B]
        gate = zb_ref[u, :, 2 * H_B * DK_B + C_B + h * DV_B:2 * H_B * DK_B + C_B + (h + 1) * DV_B]
        k = jnp.where(live, k, 0.0)
        v = jnp.where(live, v, 0.0)
        lam = math.log(1.0 - 2.0 ** (-5.0 - h))
        dmat = jnp.where(incl, jnp.exp(lam * (steps_i - steps_j)), 0.0)
        sc = _bdot_nt(q, k) * dmat
        yield
        o = _bdot(sc, v)
        kz = k * jnp.exp(lam * (cfg.valid - steps_c))
        xi = jnp.exp(lam * steps_c)
        for s in range(cfg.nseq):
            mine = seq_c == s
            st = s_scr[u * cfg.nseq + s, h]
            o = o + _pick(cfg, mine, _bdot(q, st) * xi)
            s_scr[u * cfg.nseq + s, h] = math.exp(lam * cfg.valid) * st + _bdot_tn(_pick(cfg, mine, kz), v)
        yield
        oc = o - jnp.mean(o, -1, keepdims=True)
        var = jnp.mean(oc * oc, -1, keepdims=True)
        o = oc * lax.rsqrt(var + RET_GN_EPS) * g_ref[:, sl] + b_ref[:, sl]
        o_ref[u, :, sl] = o * _silu(gate)

    _interleave(head(u, h) for u in range(UNITS) for h in range(H_B))

    @pl.when(c == cfg.nc - 1)
    def _():
        sout_ref[...] = s_scr[...]


def retention(zb, s0, gn_g, gn_b, i, cfg):
    par = ((None, 1, C_B), lambda b, c: (i, 0, 0))
    return _seq_call(_ret_kernel, cfg, "retention", [zb], [], [s0], [(gn_g,) + par, (gn_b,) + par],
                     [C_B], [], [(H_B, DK_B, DV_B)],
                     [pltpu.VMEM((UNITS * cfg.nseq, H_B, DK_B, DV_B), F32)], i)


def _gdn_kernel(*refs, cfg):
    if cfg.has_state:
        (qkv_ref, ba_ref, zg_ref, pv_ref, s0_ref, cw_ref, alog_ref, dtb_ref, ng_ref, selb_ref,
         o_ref, cvout_ref, sout_ref, s_scr, pv_scr) = refs
    else:
        (qkv_ref, ba_ref, zg_ref, cw_ref, alog_ref, dtb_ref, ng_ref, selb_ref,
         o_ref, cvout_ref, sout_ref, s_scr, pv_scr) = refs
    c = pl.program_id(1)
    rows, lb, valid = cfg.rows, cfg.lb, cfg.valid

    @pl.when(c == 0)
    def _():
        s_scr[...] = s0_ref[...] if cfg.has_state else jnp.zeros(s_scr.shape, F32)
        pv_scr[...] = pv_ref[...] if cfg.has_state else jnp.zeros(pv_scr.shape, F32)

    seq_c, t_c = _row_info(cfg, (rows, 1), 0)
    live = t_c < valid
    incl_r, _, same_r = _pair_masks(cfg, rows)
    n = GDN_GROUP * rows
    incl, strict, _ = _pair_masks(cfg, n)
    wide = lambda a: jnp.concatenate([a] * (n // LANES), axis=1)

    def prep(u):
        x = qkv_ref[u]
        hist = pv_scr[u * cfg.hist:(u + 1) * cfg.hist]
        conv = cw_ref[CONV_W - 1:CONV_W, :] * x
        for s in range(1, CONV_W):
            conv = conv + cw_ref[CONV_W - 1 - s:CONV_W - s, :] * _shifted(cfg, x, hist, t_c, s)
        pv_scr[u * cfg.hist:(u + 1) * cfg.hist] = _new_history(cfg, x)
        qkv = _silu(conv)
        ba = ba_ref[u]
        beta_all = jnp.where(live, _sigmoid(ba[:, 0:LANES]), 0.0)
        g_all = jnp.where(live, -jnp.exp(alog_ref[...]) * _softplus(ba[:, LANES:2 * LANES] + dtb_ref[...]), 0.0)
        gc_all = _mask_dot(incl_r, g_all)
        gl_all = _mask_dot(same_r, g_all)
        ex = _dot_mask(jnp.concatenate([beta_all, gc_all, gl_all], axis=0), selb_ref[...])
        return qkv, ex[0:rows], ex[rows:2 * rows], ex[2 * rows:3 * rows]

    def group(u, gi, qkv, bx, gcx, glx):
        heads = range(gi * GDN_GROUP, (gi + 1) * GDN_GROUP)

        def stack(arr, base):
            return jnp.concatenate([arr[:, base + DK_C * h:base + DK_C * (h + 1)] for h in heads], axis=0)

        def unit(a):
            return a * lax.rsqrt(jnp.maximum(jnp.sum(a * a, -1, keepdims=True), 1e-12))

        qs = unit(stack(qkv, 0)) * DK_C ** -0.5
        ks = unit(stack(qkv, H_C * DK_C))
        vs = stack(qkv, 2 * H_C * DK_C)
        bs, gcs, gls = stack(bx, 0), stack(gcx, 0), stack(glx, 0)
        gct = gcs.T
        diff = wide(gcs) - jnp.concatenate([gct] * (n // LANES), axis=0)
        dmat = jnp.where(incl, jnp.exp(jnp.where(incl, diff, 0.0)), 0.0)
        kk = _bdot_nt(ks, ks)
        qk = _bdot_nt(qs, ks) * dmat
        yield
        a = jnp.where(strict, wide(bs) * kk * dmat, 0.0)
        tinv = yield from _unit_lower_inverse(-a, lb)
        egc = jnp.exp(gcs)
        uw = _bdot(tinv, jnp.concatenate([vs * bs, ks * bs * egc], axis=1))
        u_, w = uw[:, 0:DV_C], uw[:, DV_C:DV_C + DK_C]
        yield
        qg = qs * egc
        kd = ks * jnp.exp(gls - gcs)
        egl = jnp.exp(gls)
        v_new, o_state = [], []
        for hh, h in enumerate(heads):
            rs = slice(hh * rows, (hh + 1) * rows)
            wq = jnp.concatenate([w[rs], qg[rs]], axis=0)
            ws = jnp.zeros((rows, DV_C), F32)
            os_ = jnp.zeros((rows, DV_C), F32)
            for s in range(cfg.nseq):
                res = _bdot(wq, s_scr[u * cfg.nseq + s, h])
                mine = seq_c == s
                ws = ws + _pick(cfg, mine, res[0:rows])
                os_ = os_ + _pick(cfg, mine, res[rows:2 * rows])
            v_new.append(u_[rs] - ws)
            o_state.append(os_)
        yield
        v_new_s = jnp.concatenate(v_new, axis=0)
        o_s = jnp.concatenate(o_state, axis=0) + _bdot(qk, v_new_s)
        for hh, h in enumerate(heads):
            rs = slice(hh * rows, (hh + 1) * rows)
            for s in range(cfg.nseq):
                mine = seq_c == s
                r0 = hh * rows + s * lb
                s_scr[u * cfg.nseq + s, h] = (egl[r0:r0 + 1, :] * s_scr[u * cfg.nseq + s, h]
                                              + _bdot_tn(_pick(cfg, mine, kd[rs]), v_new[hh]))
        yield
        for hh, h in enumerate(heads):
            o = o_s[hh * rows:(hh + 1) * rows]
            o = o * lax.rsqrt(jnp.mean(o * o, -1, keepdims=True) + 1e-6) * ng_ref[...]
            o_ref[u, :, h * DV_C:(h + 1) * DV_C] = o * _silu(zg_ref[u, :, h * DV_C:(h + 1) * DV_C])

    preps = [prep(u) for u in range(UNITS)]
    _interleave(group(u, gi, *preps[u]) for u in range(UNITS) for gi in range(H_C // GDN_GROUP))

    @pl.when(c == cfg.nc - 1)
    def _():
        cvout_ref[...] = pv_scr[...]
        sout_ref[...] = s_scr[...]


def gdn(qkv, ba, zg, conv_prev, s0, conv_w, a_log, dt_bias, norm_g, selb, i, cfg):
    par = lambda n: ((None, 1, n), lambda b, c: (i, 0, 0))
    params = [(conv_w, (None, CONV_W, N_QKV_C), lambda b, c: (i, 0, 0)), (a_log,) + par(LANES),
              (dt_bias,) + par(LANES), (norm_g,) + par(DV_C), (selb, selb.shape, lambda b, c: (0, 0))]
    return _seq_call(_gdn_kernel, cfg, "gdn", [qkv, ba, zg], [conv_prev], [s0], params,
                     [C_C], [N_QKV_C], [(H_C, DK_C, DV_C)],
                     [pltpu.VMEM((UNITS * cfg.nseq, H_C, DK_C, DV_C), F32),
                      pltpu.VMEM((UNITS * cfg.hist, N_QKV_C), F32)], i)


RWKV_VEC_ROWS = 8
RWKV_GL = RWKV_GROUP * N_A
RWKV_NG = H_A // RWKV_GROUP


def _rwkv_kernel(*refs, cfg):
    if cfg.has_state:
        (za_ref, pv_ref, s0_ref, mu_ref, vec_ref, lw_ref, ones_ref,
         o_ref, shout_ref, sout_ref, s_scr, pv_scr) = refs
    else:
        (za_ref, mu_ref, vec_ref, lw_ref, ones_ref, o_ref, shout_ref, sout_ref, s_scr, pv_scr) = refs
    c = pl.program_id(1)
    rows, lb, valid = cfg.rows, cfg.lb, cfg.valid
    n = RWKV_GROUP * rows
    head_shift = int(math.log2(N_A))
    row_shift = int(math.log2(rows))
    assert 1 << row_shift == rows

    bi, bj = _iota((RWKV_GL, RWKV_GL), 0), _iota((RWKV_GL, RWKV_GL), 1)
    blockdiag = (bi >> head_shift) == (bj >> head_shift)

    def slot(u, s, g):
        return (u * cfg.nseq + s) * RWKV_NG + g

    @pl.when(c == 0)
    def _():
        pv_scr[...] = pv_ref[...] if cfg.has_state else jnp.zeros(pv_scr.shape, F32)
        for u in range(UNITS):
            for s in range(cfg.nseq):
                for g in range(RWKV_NG):
                    if cfg.has_state:
                        sc = s0_ref[u * cfg.nseq + s, :, g * RWKV_GL:(g + 1) * RWKV_GL]
                        s_scr[slot(u, s, g)] = jnp.where(blockdiag, _tile_rows(sc, RWKV_GROUP), 0.0)
                    else:
                        s_scr[slot(u, s, g)] = jnp.zeros((RWKV_GL, RWKV_GL), F32)

    seq_c, t_c = _row_info(cfg, (rows, 1), 0)
    live = t_c < valid
    incl_r, _, same_r = _pair_masks(cfg, rows)
    tj = _iota((rows, rows), 1) & (lb - 1)
    sum_masks = jnp.concatenate([incl_r.astype(BF16), (same_r & (tj < lb // 2)).astype(BF16), same_r.astype(BF16)],
                                axis=0)
    hi, hl = _iota((n, RWKV_GL), 0), _iota((n, RWKV_GL), 1)
    headmask = (hi >> row_shift) == (hl >> head_shift)
    incl, strict, _ = _pair_masks(cfg, n)
    spread = lambda x: jnp.where(headmask, _tile_rows(x, RWKV_GROUP), 0.0)
    w0, a0, k_k, k_a, r_k, gn_g, gn_b = (vec_ref[j:j + 1, :] for j in range(7))
    ones_blk = ones_ref[...]

    def prep(u):
        za = za_ref[u]
        za_prev = _shifted(cfg, za, pv_scr[u * cfg.hist:(u + 1) * cfg.hist], t_c, 1)
        pv_scr[u * cfg.hist:(u + 1) * cfg.hist] = _new_history(cfg, za)
        zs = za + (za_prev - za) * mu_ref[...]
        r = zs[:, 0:C_A]
        k = zs[:, C_A:2 * C_A]
        v = zs[:, 2 * C_A:3 * C_A]
        lo = zs[:, 3 * C_A:A_SHIFT_COLS]
        ll = _iota((rows, LORA_COLS), 1)
        lo = jnp.where(ll < W_LORA, jnp.tanh(lo), jnp.where(ll < W_LORA + A_LORA, lo, _sigmoid(lo)))
        w_log = -_softplus(-(w0 + _bdot(lo, lw_ref[0]))) - 0.5
        logw = -jnp.exp(w_log)
        a = _sigmoid(a0 + _bdot(lo, lw_ref[1]))
        gate = _bdot(lo, lw_ref[2])
        kk = k * k_k
        kk = kk * lax.rsqrt(jnp.maximum(_bdot(kk * kk, ones_blk), 1e-12))
        k = k * (1.0 + (a - 1.0) * k_a)
        bonus = _bdot(r * k * r_k, ones_blk) * v
        logw = jnp.where(live, logw, 0.0)
        kk = jnp.where(live, kk, 0.0)
        k = jnp.where(live, k, 0.0)
        v = jnp.where(live, v, 0.0)
        ka = kk * a
        sums = _mask_dot(sum_masks, logw)
        cl, cm, ct = sums[0:rows], sums[rows:2 * rows], sums[2 * rows:3 * rows]
        clp = cl - logw
        e_mid = jnp.exp(cm - cl)
        e_end = jnp.exp(ct - cl)
        vecs = dict(
            b0=kk * jnp.exp(clp), r0=r * jnp.exp(cl),
            bc=kk * jnp.exp(clp - cm), rc=r * jnp.exp(cl - cm),
            ac=-ka * e_mid, kc=k * e_mid,
            ah=-ka * e_end, kh=k * e_end, wc=jnp.exp(ct), v=v)
        return vecs, bonus, gate

    def group(u, g, q, outs):
        sl = slice(g * RWKV_GL, (g + 1) * RWKV_GL)
        gram = _bdot_nt(jnp.concatenate([spread(q["bc"][:, sl]), spread(q["rc"][:, sl])], axis=0),
                        jnp.concatenate([_tile_rows(q["ac"][:, sl], RWKV_GROUP),
                                         _tile_rows(q["kc"][:, sl], RWKV_GROUP)], axis=0))
        br = jnp.concatenate([q["b0"][:, sl], q["r0"][:, sl]], axis=0)
        u0 = jnp.zeros((rows, RWKV_GL), F32)
        o0 = jnp.zeros((rows, RWKV_GL), F32)
        for s in range(cfg.nseq):
            res = _bdot_nt(br, s_scr[slot(u, s, g)])
            mine = seq_c == s
            u0 = u0 + _pick(cfg, mine, res[0:rows])
            o0 = o0 + _pick(cfg, mine, res[rows:2 * rows])
        yield
        x1 = jnp.where(strict, gram[0:n, 0:n], 0.0)
        x2 = jnp.where(strict, gram[0:n, n:2 * n], 0.0)
        x3 = jnp.where(incl, gram[n:2 * n, 0:n], 0.0)
        x4 = jnp.where(incl, gram[n:2 * n, n:2 * n], 0.0)
        vh = spread(q["v"][:, sl])
        rhs = spread(u0) + _bdot(x2, vh)
        tinv = yield from _unit_lower_inverse(x1, lb)
        uh = _bdot(tinv, rhs)
        yield
        oh = _bdot(x3, uh) + _bdot(x4, vh)
        outs[u][g] = o0 + _sum_row_blocks(oh, RWKV_GROUP)
        uv = jnp.concatenate([_sum_row_blocks(uh, RWKV_GROUP), q["v"][:, sl]], axis=0)
        akh = jnp.concatenate([q["ah"][:, sl], q["kh"][:, sl]], axis=0)
        for s in range(cfg.nseq):
            mine2 = jnp.concatenate([seq_c == s] * 2, axis=0)
            upd = _bdot_tn(_pick(cfg, mine2, uv), akh)
            s_scr[slot(u, s, g)] = (s_scr[slot(u, s, g)] * q["wc"][s * lb:s * lb + 1, sl]
                                    + jnp.where(blockdiag, upd, 0.0))
        yield

    preps = [prep(u) for u in range(UNITS)]
    outs = [[None] * RWKV_NG for _ in range(UNITS)]
    _interleave(group(u, g, preps[u][0], outs) for u in range(UNITS) for g in range(RWKV_NG))
    for u in range(UNITS):
        _, bonus, gate = preps[u]
        o = jnp.concatenate(outs[u], axis=1)
        mean = _bdot(o, ones_blk) * (1.0 / N_A)
        oc = o - mean
        var = _bdot(oc * oc, ones_blk) * (1.0 / N_A)
        o = oc * lax.rsqrt(var + RWKV_GN_EPS) * gn_g + gn_b
        o_ref[u] = (o + bonus) * gate

    @pl.when(c == cfg.nc - 1)
    def _():
        shout_ref[...] = pv_scr[...]
        for u in range(UNITS):
            for s in range(cfg.nseq):
                for g in range(RWKV_NG):
                    sout_ref[u * cfg.nseq + s, :, g * RWKV_GL:(g + 1) * RWKV_GL] = _sum_row_blocks(
                        s_scr[slot(u, s, g)], RWKV_GROUP)


def rwkv(za, shift_prev, s0, mu, vecs, lora_w, ones_blk, i, cfg):
    params = [(mu, (None, 1, A_SHIFT_COLS), lambda b, c: (i, 0, 0)),
              (vecs, (None, RWKV_VEC_ROWS, C_A), lambda b, c: (i, 0, 0)),
              (lora_w, (None, 3, LORA_COLS, C_A), lambda b, c: (i, 0, 0, 0)),
              (ones_blk, (C_A, C_A), lambda b, c: (0, 0))]
    return _seq_call(_rwkv_kernel, cfg, "rwkv", [za], [shift_prev], [s0], params,
                     [C_A], [A_SHIFT_COLS], [(N_A, C_A)],
                     [pltpu.VMEM((UNITS * cfg.nseq * RWKV_NG, RWKV_GL, RWKV_GL), F32),
                      pltpu.VMEM((UNITS * cfg.hist, A_SHIFT_COLS), F32)], i)


def _trunk(x, cfg_rwkv, cfg_ret, cfg_gdn, st, p):
    has = st is not None
    h = x
    new = dict(shift=[], rwkv=[], ret=[], conv=[], gdn=[])
    for l in range(DEPTH):
        i = l // 2
        h = ffn_ln(h, p["wg"], p["wu"], p["wd"], p["ln_g"], p["ln_b"], l, 0, 0)
        if l % 2 == 0:
            za, zb = project(h, p["ab_in"], i, (A_SHIFT_COLS, B_COLS))
            o_a, sh, s_rw = rwkv(za, st["shift"] if has else None, st["rwkv"] if has else None,
                                 p["mu"], p["rwkv_vecs"], p["lora_w"], p["ones_blk"], i, cfg_rwkv)
            o_b, s_rt = retention(zb, st["ret"] if has else None, p["ret_g"], p["ret_b"], i, cfg_ret)
            h = out_ln(h, [o_a, o_b], p["ab_out"], i, p["ln_g"], p["ln_b"], l)
            new["shift"].append(sh)
            new["rwkv"].append(s_rw)
            new["ret"].append(s_rt)
        else:
            qkv, ba, zg = project(h, p["gdn_in"], i, (N_QKV_C, 2 * LANES, C_C))
            o, cv, s_gd = gdn(qkv, ba, zg, st["conv"] if has else None, st["gdn"] if has else None,
                              p["conv_w"], p["a_log"], p["dt_bias"], p["norm_g"], p["selb"], i, cfg_gdn)
            h = out_ln(h, [o], p["gdn_out"], i, p["ln_g"], p["ln_b"], l)
            new["conv"].append(cv)
            new["gdn"].append(s_gd)
        h = ffn_ln(h, p["wg"], p["wu"], p["wd"], p["ln_g"], p["ln_b"], l, 1, 2)
    return h, {k: jnp.stack(v) for k, v in new.items()}


def _lane_pad(x, n):
    return jnp.pad(x, [(0, 0)] * (x.ndim - 1) + [(0, n - x.shape[-1])])


def _history_tiles(x):
    nl, b, r, n = x.shape
    return jnp.pad(x, ((0, 0), (0, 0), (SUBLANES - r, 0), (0, 0))).reshape(nl, b * SUBLANES, n)


def kernel(x_prompt, x_sample, state_rwkv_shift, state_rwkv, state_ret, state_gdn_conv, state_gdn, ln_g, ln_b, ffn_w_gate, ffn_w_up, ffn_w_down, ab_w_in, ab_w_out, rwkv_mu, rwkv_w0, rwkv_w2, rwkv_a0, rwkv_a2, rwkv_g2, rwkv_k_k, rwkv_k_a, rwkv_r_k, rwkv_gn_g, rwkv_gn_b, ret_gn_g, ret_gn_b, gdn_w_in, gdn_conv_w, gdn_A_log, gdn_dt_bias, gdn_norm_g, gdn_w_out):
    nb, seq = x_prompt.shape[0], x_prompt.shape[1]
    ndec, dec_seq = x_sample.shape[0], x_sample.shape[1]
    nl = N_AB_LAYERS

    lora_w = jnp.zeros((nl, 3, LORA_COLS, C_A), F32)
    lora_w = (lora_w.at[:, 0, 0:W_LORA].set(rwkv_w2).at[:, 1, W_LORA:W_LORA + A_LORA].set(rwkv_a2)
              .at[:, 2, W_LORA + A_LORA:].set(rwkv_g2))
    gdn_in = jnp.concatenate([gdn_w_in[:, :, :N_QKV_C],
                              _lane_pad(gdn_w_in[:, :, N_QKV_C:N_QKV_C + H_C], LANES),
                              _lane_pad(gdn_w_in[:, :, N_QKV_C + H_C:N_QKV_C + 2 * H_C], LANES),
                              gdn_w_in[:, :, N_QKV_C + 2 * H_C:]], axis=2)
    lane_head = np.arange(C_A) // N_A
    sel = np.zeros((LANES, H_C * LANES), np.float32)
    for hd in range(H_C):
        sel[hd, hd * LANES:(hd + 1) * LANES] = 1.0
    p = dict(
        wg=ffn_w_gate.astype(BF16), wu=ffn_w_up.astype(BF16), wd=ffn_w_down.astype(BF16),
        ln_g=ln_g.reshape(DEPTH * 3, 1, D_MODEL), ln_b=ln_b.reshape(DEPTH * 3, 1, D_MODEL),
        ab_in=ab_w_in.astype(BF16), ab_out=ab_w_out.astype(BF16),
        gdn_in=gdn_in.astype(BF16), gdn_out=gdn_w_out.astype(BF16),
        mu=rwkv_mu.reshape(nl, 1, A_SHIFT_COLS),
        rwkv_vecs=jnp.stack([rwkv_w0, rwkv_a0, rwkv_k_k, rwkv_k_a, rwkv_r_k.reshape(nl, C_A), rwkv_gn_g, rwkv_gn_b,
                             jnp.zeros_like(rwkv_w0)], axis=1),
        lora_w=lora_w.astype(BF16),
        ones_blk=jnp.asarray((lane_head[:, None] == lane_head[None, :]).astype(np.float32)),
        ret_g=ret_gn_g.reshape(nl, 1, C_B), ret_b=ret_gn_b.reshape(nl, 1, C_B),
        conv_w=gdn_conv_w, a_log=_lane_pad(gdn_A_log, LANES)[:, None, :], dt_bias=_lane_pad(gdn_dt_bias, LANES)[:, None, :],
        norm_g=gdn_norm_g[:, None, :], selb=jnp.asarray(sel),
    )

    hp, sp = _trunk(x_prompt.reshape(nb * seq, D_MODEL),
                    SeqCfg(1, 64, 64, seq // 64, 0, False), SeqCfg(1, 128, 128, seq // 128, 0, False),
                    SeqCfg(1, 64, 64, seq // 64, 0, False), None, p)
    cfg = SeqCfg(SEQ_PER_UNIT, SAMPLE_PAD, dec_seq, 1, PAST_LEN, True)
    xs = jnp.pad(x_sample, ((0, 0), (0, SAMPLE_PAD - dec_seq), (0, 0))).reshape(ndec * SAMPLE_PAD, D_MODEL)
    st = dict(shift=_history_tiles(state_rwkv_shift[:, :, None, :]),
              rwkv=state_rwkv.transpose(0, 1, 3, 2, 4).reshape(nl, ndec, N_A, C_A),
              ret=state_ret,
              conv=_history_tiles(state_gdn_conv),
              gdn=state_gdn)
    hs, ss = _trunk(xs, cfg, cfg, cfg, st, p)

    def unpack(s, n):
        return (s["shift"].reshape(nl, n, SUBLANES, A_SHIFT_COLS)[:, :, SUBLANES - 1],
                s["rwkv"].reshape(nl, n, N_A, H_A, N_A).transpose(0, 1, 3, 2, 4),
                s["ret"],
                s["conv"].reshape(N_C_LAYERS, n, SUBLANES, N_QKV_C)[:, :, SUBLANES - (CONV_W - 1):],
                s["gdn"])

    y_prompt = hp.reshape(nb, seq, D_MODEL)
    y_sample = hs.reshape(ndec, SAMPLE_PAD, D_MODEL)[:, :dec_seq]
    return (y_prompt, y_sample) + unpack(sp, nb) + unpack(ss, ndec)
```

```python
import functools
import math

import numpy as np
import jax
import jax.numpy as jnp
from jax import lax
from jax.experimental import pallas as pl
from jax.experimental.pallas import tpu as pltpu

F32 = jnp.float32
BF16 = jnp.bfloat16

D_MODEL = 1024
DEPTH = 4
PAST_LEN = 16384
N_AB_LAYERS = (DEPTH + 1) // 2
N_C_LAYERS = DEPTH // 2
H_A, N_A = 8, 64
C_A = H_A * N_A
W_LORA, A_LORA, G_LORA = 64, 64, 128
LORA_COLS = W_LORA + A_LORA + G_LORA
A_SHIFT_COLS = 3 * C_A + LORA_COLS
RWKV_GN_EPS = 64e-5
H_B, DK_B, DV_B = 4, 128, 128
C_B = H_B * DV_B
ROPE_THETA = 10000.0
RET_GN_EPS = 1e-5
B_COLS = 2 * H_B * DK_B + 2 * C_B
AB_COLS = A_SHIFT_COLS + B_COLS
H_C, DK_C, DV_C = 8, 128, 128
C_C = H_C * DV_C
CONV_W = 4
N_QKV_C = 2 * H_C * DK_C + C_C
D_FF = 2816
LN_EPS = 1e-5
DEEPNORM_ALPHA = (2 * DEPTH) ** 0.25

V7X_VMEM_BYTES = 64 * 1024 * 1024
LANES = 128
SUBLANES = 8
VMEM_LIMIT = V7X_VMEM_BYTES * 7 // 8

SAMPLE_PAD = SUBLANES
SEQ_PER_UNIT = 8
UNITS = 2
UNITS_DEEP = 4
ROW_TILE = 512
RWKV_GROUP = 4
GDN_GROUP = 4


class SeqCfg:
    def __init__(self, nseq, lb, valid, nc, pos0, has_state, units):
        self.nseq, self.lb, self.valid, self.nc, self.pos0, self.has_state = nseq, lb, valid, nc, pos0, has_state
        self.units = units
        self.rows = nseq * lb
        self.shift = int(math.log2(lb))
        self.hist = nseq * SUBLANES
        assert 1 << self.shift == lb and lb >= SUBLANES and CONV_W - 1 <= valid <= lb


def _cparams(ngrid):
    return pltpu.CompilerParams(dimension_semantics=("arbitrary",) * ngrid, vmem_limit_bytes=VMEM_LIMIT)


def _bdot(a, b):
    return jnp.dot(a.astype(BF16), b.astype(BF16), preferred_element_type=F32)


def _bdot_nt(a, b):
    return lax.dot_general(a.astype(BF16), b.astype(BF16), (((1,), (1,)), ((), ())), preferred_element_type=F32)


def _bdot_tn(a, b):
    return lax.dot_general(a.astype(BF16), b.astype(BF16), (((0,), (0,)), ((), ())), preferred_element_type=F32)


def _split3(x):
    x1 = x.astype(BF16)
    r1 = x - x1.astype(F32)
    x2 = r1.astype(BF16)
    return x1, x2, (r1 - x2.astype(F32)).astype(BF16)


def _mask_dot(m, x):
    mb = m.astype(BF16)
    x1, x2, x3 = _split3(x)
    return (jnp.dot(mb, x1, preferred_element_type=F32) + jnp.dot(mb, x2, preferred_element_type=F32)
            + jnp.dot(mb, x3, preferred_element_type=F32))


def _sigmoid(x):
    return 1.0 / (1.0 + jnp.exp(-x))


def _silu(x):
    return x * _sigmoid(x)


def _softplus(x):
    return jnp.maximum(x, 0.0) + jnp.log(1.0 + jnp.exp(-jnp.abs(x)))


def _layer_norm(y, g, b):
    yc = y - jnp.mean(y, -1, keepdims=True)
    var = jnp.mean(yc * yc, -1, keepdims=True)
    return yc * lax.rsqrt(var + LN_EPS) * g + b


def _iota(shape, dim):
    return lax.broadcasted_iota(jnp.int32, shape, dim)


def _tile_rows(x, n):
    return jnp.concatenate([x] * n, axis=0)


def _sum_row_blocks(x, n):
    r = x.shape[0] // n
    out = x[0:r]
    for i in range(1, n):
        out = out + x[i * r:(i + 1) * r]
    return out


def _unit_lower_inverse(x, lb):
    n = x.shape[0]
    eye = (_iota((n, n), 0) == _iota((n, n), 1)).astype(F32)
    p = eye + x
    rounds = int(math.log2(lb)) - 1
    y = _bdot(x, x)
    yield
    for r in range(rounds):
        if r == rounds - 1:
            p = p + _bdot(p, y)
        else:
            z = _bdot(jnp.concatenate([p, y], axis=0), y)
            p = p + z[0:n]
            y = z[n:2 * n]
        yield
    return p


def _head_sums(x, ones_pair):
    rows = x.shape[0]
    nblk = x.shape[1] // LANES
    st = jnp.concatenate([x[:, j * LANES:(j + 1) * LANES] for j in range(nblk)], axis=0)
    s = _bdot(st, ones_pair)
    return jnp.concatenate([s[j * rows:(j + 1) * rows] for j in range(nblk)], axis=1)


def _interleave(gens):
    gens = list(gens)
    while gens:
        for g in list(gens):
            try:
                next(g)
            except StopIteration:
                gens.remove(g)


def _token_kernel(*refs, n_mix):
    it = iter(refs)
    x = next(it)[...]
    if n_mix:
        o_refs = [next(it) for _ in range(n_mix)]
        w_ref, g_ref, b_ref = next(it), next(it), next(it)
        acc = DEEPNORM_ALPHA * x
        off = 0
        for o_ref in o_refs:
            n = o_ref.shape[1]
            acc = acc + jnp.dot(o_ref[...].astype(BF16), w_ref[off:off + n, :], preferred_element_type=F32)
            off += n
        x = _layer_norm(acc, g_ref[...], b_ref[...])
    wg_ref, wu_ref, wd_ref, g_ref, b_ref, out_ref = (next(it) for _ in range(6))
    xb = x.astype(BF16)
    gate = jnp.dot(xb, wg_ref[...], preferred_element_type=F32)
    up = jnp.dot(xb, wu_ref[...], preferred_element_type=F32)
    act = (_silu(gate) * up).astype(BF16)
    f = jnp.dot(act, wd_ref[...], preferred_element_type=F32)
    out_ref[...] = _layer_norm(DEEPNORM_ALPHA * x + 0.5 * f, g_ref[...], b_ref[...])


def _resident(block_shape, index_map):
    return pl.BlockSpec(block_shape, index_map, pipeline_mode=pl.Buffered(1))


def token_block(h, p, l, j, mix=None):
    t = h.shape[0]
    row = lambda n: pl.BlockSpec((ROW_TILE, n), lambda r: (r, 0))
    ln = lambda k: pl.BlockSpec((None, 1, D_MODEL), lambda r: (3 * l + k, 0, 0))
    args, in_specs = [h], [row(D_MODEL)]
    if mix:
        outs, w_out, i = mix
        args += list(outs) + [w_out, p["ln_g"], p["ln_b"]]
        in_specs += [row(o.shape[1]) for o in outs] + [_resident((None, D_MODEL, D_MODEL), lambda r: (i, 0, 0)), ln(1), ln(1)]
    args += [p["wg"], p["wu"], p["wd"], p["ln_g"], p["ln_b"]]
    in_specs += [_resident((None, None, D_MODEL, D_FF), lambda r: (l, j, 0, 0)),
                 _resident((None, None, D_MODEL, D_FF), lambda r: (l, j, 0, 0)),
                 _resident((None, None, D_FF, D_MODEL), lambda r: (l, j, 0, 0)), ln(2 * j), ln(2 * j)]
    return pl.pallas_call(
        functools.partial(_token_kernel, n_mix=len(mix[0]) if mix else 0),
        grid=(t // ROW_TILE,),
        in_specs=in_specs,
        out_specs=row(D_MODEL),
        out_shape=jax.ShapeDtypeStruct((t, D_MODEL), F32),
        compiler_params=_cparams(1),
        name="token_block",
    )(*args)


def _proj_kernel(h_ref, w_ref, *o_refs):
    xb = h_ref[...].astype(BF16)
    off = 0
    for o_ref in o_refs:
        n = o_ref.shape[1]
        o_ref[...] = jnp.dot(xb, w_ref[:, off:off + n], preferred_element_type=F32)
        off += n


def project(h, w, i, widths):
    t = h.shape[0]
    ncols = w.shape[2]
    assert sum(widths) == ncols
    return pl.pallas_call(
        _proj_kernel,
        grid=(t // ROW_TILE,),
        in_specs=[pl.BlockSpec((ROW_TILE, D_MODEL), lambda r: (r, 0)),
                  _resident((None, D_MODEL, ncols), lambda r: (i, 0, 0))],
        out_specs=[pl.BlockSpec((ROW_TILE, n), lambda r: (r, 0)) for n in widths],
        out_shape=[jax.ShapeDtypeStruct((t, n), F32) for n in widths],
        compiler_params=_cparams(1),
        name="project",
    )(h, w)


def _row_info(cfg, shape, dim):
    r = _iota(shape, dim)
    return r >> cfg.shift, r & (cfg.lb - 1)


def _pick(cfg, mine, x):
    return x if cfg.nseq == 1 else jnp.where(mine, x, 0.0)


def _pair_masks(cfg, n):
    ri, ci = _iota((n, n), 0), _iota((n, n), 1)
    same = (ri >> cfg.shift) == (ci >> cfg.shift)
    ti, tj = ri & (cfg.lb - 1), ci & (cfg.lb - 1)
    return same & (tj <= ti), same & (tj < ti), same


def _shifted(cfg, x, hist, t_c, s):
    rolled = pltpu.roll(x, s, 0)
    if cfg.nseq == 1:
        head = jnp.where(t_c[0:SUBLANES] >= s, rolled[0:SUBLANES], pltpu.roll(hist, s, 0))
        return jnp.concatenate([head, rolled[SUBLANES:]], axis=0)
    assert cfg.hist == cfg.rows
    return jnp.where(t_c >= s, rolled, pltpu.roll(hist, cfg.rows - SUBLANES + s, 0))


def _new_history(cfg, x):
    rolled = x if cfg.lb == cfg.valid else pltpu.roll(x, cfg.lb - cfg.valid, 0)
    return jnp.concatenate([rolled[(j + 1) * cfg.lb - SUBLANES:(j + 1) * cfg.lb] for j in range(cfg.nseq)], axis=0)


def _seq_call(kernel, cfg, name, rows_in, hist_in, state_in, params, rows_out, hist_out, state_out, scratch, i):
    t = rows_in[0].shape[0]
    nbt = t // (cfg.rows * cfg.nc)
    assert nbt % cfg.units == 0
    grid = (nbt // cfg.units, cfg.nc)

    def zeros(n):
        return (0,) * n

    row_spec = lambda n: pl.BlockSpec((cfg.units, None, cfg.rows, n), lambda b, c: (b, c, 0, 0))
    args = [a.reshape(nbt, cfg.nc, cfg.rows, a.shape[1]) for a in rows_in]
    in_specs = [row_spec(a.shape[1]) for a in rows_in]
    if cfg.has_state:
        for a in hist_in:
            args.append(a)
            in_specs.append(pl.BlockSpec((None, cfg.units * cfg.hist, a.shape[2]), lambda b, c: (i, b, 0)))
        for a in state_in:
            nd = a.ndim - 2
            args.append(a)
            in_specs.append(pl.BlockSpec((None, cfg.units * cfg.nseq) + a.shape[2:],
                                         lambda b, c, nd=nd: (i, b) + zeros(nd)))
    for a, bs, im in params:
        args.append(a)
        in_specs.append(pl.BlockSpec(bs, im))
    out_specs = [row_spec(n) for n in rows_out]
    out_shape = [jax.ShapeDtypeStruct((nbt, cfg.nc, cfg.rows, n), F32) for n in rows_out]
    for n in hist_out:
        out_specs.append(pl.BlockSpec((cfg.units * cfg.hist, n), lambda b, c: (b, 0)))
        out_shape.append(jax.ShapeDtypeStruct((nbt * cfg.hist, n), F32))
    for shp in state_out:
        out_specs.append(pl.BlockSpec((cfg.units * cfg.nseq,) + shp, lambda b, c, nd=len(shp): (b,) + zeros(nd)))
        out_shape.append(jax.ShapeDtypeStruct((nbt * cfg.nseq,) + shp, F32))
    outs = pl.pallas_call(
        functools.partial(kernel, cfg=cfg),
        grid=grid, in_specs=in_specs, out_specs=out_specs, out_shape=out_shape,
        scratch_shapes=scratch, compiler_params=_cparams(2), name=name,
    )(*args)
    return [o.reshape(t, o.shape[-1]) for o in outs[:len(rows_out)]] + list(outs[len(rows_out):])


def _ret_kernel(*refs, cfg):
    if cfg.has_state:
        zb_ref, s0_ref, g_ref, b_ref, o_ref, sout_ref, s_scr = refs
    else:
        zb_ref, g_ref, b_ref, o_ref, sout_ref, s_scr = refs
    c = pl.program_id(1)
    rows = cfg.rows

    @pl.when(c == 0)
    def _():
        s_scr[...] = s0_ref[...] if cfg.has_state else jnp.zeros(s_scr.shape, F32)

    seq_c, t_c = _row_info(cfg, (rows, 1), 0)
    live = t_c < cfg.valid
    incl, _, _ = _pair_masks(cfg, rows)
    ti = (_iota((rows, rows), 0) & (cfg.lb - 1))
    tj = (_iota((rows, rows), 1) & (cfg.lb - 1))
    steps_i = jnp.minimum(ti + 1, cfg.valid).astype(F32)
    steps_j = jnp.minimum(tj + 1, cfg.valid).astype(F32)
    steps_c = jnp.minimum(t_c + 1, cfg.valid).astype(F32)

    lane = _iota((rows, LANES), 1)
    half = DK_B // 2
    inv_freq = 1.0 / jnp.power(ROPE_THETA, (lane & (half - 1)).astype(F32) / (half - 1.0))
    pos = (cfg.pos0 + c * cfg.lb + t_c).astype(F32)
    ang = pos * inv_freq
    cos = jnp.cos(ang)
    sin = jnp.sin(ang)
    sin = jnp.where(lane < half, -sin, sin)

    def rot(x):
        return x * cos + pltpu.roll(x, half, 1) * sin

    def head(u, h):
        sl = slice(h * DK_B, (h + 1) * DK_B)
        q = rot(zb_ref[u, :, sl])
        k = rot(zb_ref[u, :, H_B * DK_B + h * DK_B:H_B * DK_B + (h + 1) * DK_B]) * DK_B ** -0.5
        v = zb_ref[u, :, 2 * H_B * DK_B + h * DV_B:2 * H_B * DK_B + (h + 1) * DV_B]
        gate = zb_ref[u, :, 2 * H_B * DK_B + C_B + h * DV_B:2 * H_B * DK_B + C_B + (h + 1) * DV_B]
        k = jnp.where(live, k, 0.0)
        v = jnp.where(live, v, 0.0)
        lam = math.log(1.0 - 2.0 ** (-5.0 - h))
        dmat = jnp.where(incl, jnp.exp(lam * (steps_i - steps_j)), 0.0)
        sc = _bdot_nt(q, k) * dmat
        yield
        o = _bdot(sc, v)
        kz = k * jnp.exp(lam * (cfg.valid - steps_c))
        xi = jnp.exp(lam * steps_c)
        for s in range(cfg.nseq):
            mine = seq_c == s
            st = s_scr[u * cfg.nseq + s, h]
            o = o + _pick(cfg, mine, _bdot(q, st) * xi)
            s_scr[u * cfg.nseq + s, h] = math.exp(lam * cfg.valid) * st + _bdot_tn(_pick(cfg, mine, kz), v)
        yield
        oc = o - jnp.mean(o, -1, keepdims=True)
        var = jnp.mean(oc * oc, -1, keepdims=True)
        o = oc * lax.rsqrt(var + RET_GN_EPS) * g_ref[:, sl] + b_ref[:, sl]
        o_ref[u, :, sl] = o * _silu(gate)

    _interleave(head(u, h) for u in range(cfg.units) for h in range(H_B))

    @pl.when(c == cfg.nc - 1)
    def _():
        sout_ref[...] = s_scr[...]


def retention(zb, s0, gn_g, gn_b, i, cfg):
    par = ((None, 1, C_B), lambda b, c: (i, 0, 0))
    return _seq_call(_ret_kernel, cfg, "retention", [zb], [], [s0], [(gn_g,) + par, (gn_b,) + par],
                     [C_B], [], [(H_B, DK_B, DV_B)],
                     [pltpu.VMEM((cfg.units * cfg.nseq, H_B, DK_B, DV_B), F32)], i)


def _gdn_kernel(*refs, cfg):
    if cfg.has_state:
        (qkv_ref, ba_ref, zg_ref, pv_ref, s0_ref, cw_ref, alog_ref, dtb_ref, ng_ref,
         o_ref, cvout_ref, sout_ref, s_scr, pv_scr) = refs
    else:
        (qkv_ref, ba_ref, zg_ref, cw_ref, alog_ref, dtb_ref, ng_ref,
         o_ref, cvout_ref, sout_ref, s_scr, pv_scr) = refs
    c = pl.program_id(1)
    rows, lb, valid = cfg.rows, cfg.lb, cfg.valid

    @pl.when(c == 0)
    def _():
        s_scr[...] = s0_ref[...] if cfg.has_state else jnp.zeros(s_scr.shape, F32)
        pv_scr[...] = pv_ref[...] if cfg.has_state else jnp.zeros(pv_scr.shape, F32)

    seq_c, t_c = _row_info(cfg, (rows, 1), 0)
    live = t_c < valid
    incl_r, _, same_r = _pair_masks(cfg, rows)
    n = GDN_GROUP * rows
    incl, strict, _ = _pair_masks(cfg, n)

    def prep(u):
        x = qkv_ref[u]
        hist = pv_scr[u * cfg.hist:(u + 1) * cfg.hist]
        conv = cw_ref[CONV_W - 1:CONV_W, :] * x
        for s in range(1, CONV_W):
            conv = conv + cw_ref[CONV_W - 1 - s:CONV_W - s, :] * _shifted(cfg, x, hist, t_c, s)
        pv_scr[u * cfg.hist:(u + 1) * cfg.hist] = _new_history(cfg, x)
        qkv = _silu(conv)
        ba = ba_ref[u]
        beta_all = jnp.where(live, _sigmoid(ba[:, 0:LANES]), 0.0)
        g_all = jnp.where(live, -jnp.exp(alog_ref[...]) * _softplus(ba[:, LANES:2 * LANES] + dtb_ref[...]), 0.0)
        return qkv, beta_all, _mask_dot(incl_r, g_all), _mask_dot(same_r, g_all)

    def group(u, gi, qkv, beta_all, gc_all, gl_all):
        heads = range(gi * GDN_GROUP, (gi + 1) * GDN_GROUP)

        def stack(arr, base):
            return jnp.concatenate([arr[:, base + DK_C * h:base + DK_C * (h + 1)] for h in heads], axis=0)

        def stack_col(arr):
            return jnp.concatenate([arr[:, h:h + 1] for h in heads], axis=0)

        def unit(a):
            return a * lax.rsqrt(jnp.maximum(jnp.sum(a * a, -1, keepdims=True), 1e-12))

        qs = unit(stack(qkv, 0)) * DK_C ** -0.5
        ks = unit(stack(qkv, H_C * DK_C))
        vs = stack(qkv, 2 * H_C * DK_C)
        bs, gcs, gls = stack_col(beta_all), stack_col(gc_all), stack_col(gl_all)
        diff = gcs - gcs.T
        dmat = jnp.where(incl, jnp.exp(jnp.where(incl, diff, 0.0)), 0.0)
        kk = _bdot_nt(ks, ks)
        qk = _bdot_nt(qs, ks) * dmat
        yield
        a = jnp.where(strict, bs * kk * dmat, 0.0)
        tinv = yield from _unit_lower_inverse(-a, lb)
        egc = jnp.exp(gcs)
        uw = _bdot(tinv, jnp.concatenate([vs * bs, ks * (bs * egc)], axis=1))
        u_, w = uw[:, 0:DV_C], uw[:, DV_C:DV_C + DK_C]
        yield
        qg = qs * egc
        kd = ks * jnp.exp(gls - gcs)
        egl = jnp.exp(gls)
        v_new, o_state = [], []
        for hh, h in enumerate(heads):
            rs = slice(hh * rows, (hh + 1) * rows)
            wq = jnp.concatenate([w[rs], qg[rs]], axis=0)
            ws = jnp.zeros((rows, DV_C), F32)
            os_ = jnp.zeros((rows, DV_C), F32)
            for s in range(cfg.nseq):
                res = _bdot(wq, s_scr[u * cfg.nseq + s, h])
                mine = seq_c == s
                ws = ws + _pick(cfg, mine, res[0:rows])
                os_ = os_ + _pick(cfg, mine, res[rows:2 * rows])
            v_new.append(u_[rs] - ws)
            o_state.append(os_)
        yield
        v_new_s = jnp.concatenate(v_new, axis=0)
        o_s = jnp.concatenate(o_state, axis=0) + _bdot(qk, v_new_s)
        for hh, h in enumerate(heads):
            rs = slice(hh * rows, (hh + 1) * rows)
            for s in range(cfg.nseq):
                mine = seq_c == s
                r0 = hh * rows + s * lb
                s_scr[u * cfg.nseq + s, h] = (egl[r0:r0 + 1, :] * s_scr[u * cfg.nseq + s, h]
                                              + _bdot_tn(_pick(cfg, mine, kd[rs]), v_new[hh]))
        yield
        for hh, h in enumerate(heads):
            o = o_s[hh * rows:(hh + 1) * rows]
            o = o * lax.rsqrt(jnp.mean(o * o, -1, keepdims=True) + 1e-6) * ng_ref[...]
            o_ref[u, :, h * DV_C:(h + 1) * DV_C] = o * _silu(zg_ref[u, :, h * DV_C:(h + 1) * DV_C])

    preps = [prep(u) for u in range(cfg.units)]
    _interleave(group(u, gi, *preps[u]) for u in range(cfg.units) for gi in range(H_C // GDN_GROUP))

    @pl.when(c == cfg.nc - 1)
    def _():
        cvout_ref[...] = pv_scr[...]
        sout_ref[...] = s_scr[...]


def gdn(qkv, ba, zg, conv_prev, s0, conv_w, a_log, dt_bias, norm_g, i, cfg):
    par = lambda n: ((None, 1, n), lambda b, c: (i, 0, 0))
    params = [(conv_w, (None, CONV_W, N_QKV_C), lambda b, c: (i, 0, 0)), (a_log,) + par(LANES),
              (dt_bias,) + par(LANES), (norm_g,) + par(DV_C)]
    return _seq_call(_gdn_kernel, cfg, "gdn", [qkv, ba, zg], [conv_prev], [s0], params,
                     [C_C], [N_QKV_C], [(H_C, DK_C, DV_C)],
                     [pltpu.VMEM((cfg.units * cfg.nseq, H_C, DK_C, DV_C), F32),
                      pltpu.VMEM((cfg.units * cfg.hist, N_QKV_C), F32)], i)


RWKV_VEC_ROWS = 8
RWKV_GL = RWKV_GROUP * N_A
RWKV_NG = H_A // RWKV_GROUP


def _rwkv_kernel(*refs, cfg):
    if cfg.has_state:
        (za_ref, pv_ref, s0_ref, mu_ref, vec_ref, lw_ref, ones_ref,
         o_ref, shout_ref, sout_ref, s_scr, pv_scr) = refs
    else:
        (za_ref, mu_ref, vec_ref, lw_ref, ones_ref, o_ref, shout_ref, sout_ref, s_scr, pv_scr) = refs
    c = pl.program_id(1)
    rows, lb, valid = cfg.rows, cfg.lb, cfg.valid
    n = RWKV_GROUP * rows
    head_shift = int(math.log2(N_A))
    row_shift = int(math.log2(rows))
    assert 1 << row_shift == rows

    bi, bj = _iota((RWKV_GL, RWKV_GL), 0), _iota((RWKV_GL, RWKV_GL), 1)
    blockdiag = (bi >> head_shift) == (bj >> head_shift)

    def slot(u, s, g):
        return (u * cfg.nseq + s) * RWKV_NG + g

    @pl.when(c == 0)
    def _():
        pv_scr[...] = pv_ref[...] if cfg.has_state else jnp.zeros(pv_scr.shape, F32)
        for u in range(cfg.units):
            for s in range(cfg.nseq):
                for g in range(RWKV_NG):
                    if cfg.has_state:
                        sc = s0_ref[u * cfg.nseq + s, :, g * RWKV_GL:(g + 1) * RWKV_GL]
                        s_scr[slot(u, s, g)] = jnp.where(blockdiag, _tile_rows(sc, RWKV_GROUP), 0.0)
                    else:
                        s_scr[slot(u, s, g)] = jnp.zeros((RWKV_GL, RWKV_GL), F32)

    seq_c, t_c = _row_info(cfg, (rows, 1), 0)
    live = t_c < valid
    incl_r, _, same_r = _pair_masks(cfg, rows)
    tj = _iota((rows, rows), 1) & (lb - 1)
    sum_masks = jnp.concatenate([incl_r.astype(BF16), (same_r & (tj < lb // 2)).astype(BF16), same_r.astype(BF16)],
                                axis=0)
    hi, hl = _iota((n, RWKV_GL), 0), _iota((n, RWKV_GL), 1)
    headmask = (hi >> row_shift) == (hl >> head_shift)
    incl, strict, _ = _pair_masks(cfg, n)
    spread = lambda x: jnp.where(headmask, _tile_rows(x, RWKV_GROUP), 0.0)
    w0, a0, k_k, k_a, r_k, gn_g, gn_b = (vec_ref[j:j + 1, :] for j in range(7))
    ones_pair = ones_ref[...]

    def prep(u):
        za = za_ref[u]
        za_prev = _shifted(cfg, za, pv_scr[u * cfg.hist:(u + 1) * cfg.hist], t_c, 1)
        pv_scr[u * cfg.hist:(u + 1) * cfg.hist] = _new_history(cfg, za)
        zs = za + (za_prev - za) * mu_ref[...]
        r = zs[:, 0:C_A]
        k = zs[:, C_A:2 * C_A]
        v = zs[:, 2 * C_A:3 * C_A]
        lo = zs[:, 3 * C_A:A_SHIFT_COLS]
        ll = _iota((rows, LORA_COLS), 1)
        lo = jnp.where(ll < W_LORA, jnp.tanh(lo), jnp.where(ll < W_LORA + A_LORA, lo, _sigmoid(lo)))
        w_log = -_softplus(-(w0 + _bdot(lo, lw_ref[0]))) - 0.5
        logw = -jnp.exp(w_log)
        a = _sigmoid(a0 + _bdot(lo, lw_ref[1]))
        gate = _bdot(lo, lw_ref[2])
        kk = k * k_k
        k = k * (1.0 + (a - 1.0) * k_a)
        sums = _head_sums(jnp.concatenate([kk * kk, r * k * r_k], axis=0), ones_pair)
        kk = kk * lax.rsqrt(jnp.maximum(sums[0:rows], 1e-12))
        bonus = sums[rows:2 * rows] * v
        logw = jnp.where(live, logw, 0.0)
        kk = jnp.where(live, kk, 0.0)
        k = jnp.where(live, k, 0.0)
        v = jnp.where(live, v, 0.0)
        ka = kk * a
        sums = _mask_dot(sum_masks, logw)
        cl, cm, ct = sums[0:rows], sums[rows:2 * rows], sums[2 * rows:3 * rows]
        clp = cl - logw
        e_mid = jnp.exp(cm - cl)
        e_end = jnp.exp(ct - cl)
        vecs = dict(
            b0=kk * jnp.exp(clp), r0=r * jnp.exp(cl),
            bc=kk * jnp.exp(clp - cm), rc=r * jnp.exp(cl - cm),
            ac=-ka * e_mid, kc=k * e_mid,
            ah=-ka * e_end, kh=k * e_end, wc=jnp.exp(ct), v=v)
        return vecs, bonus, gate

    def group(u, g, q, outs):
        sl = slice(g * RWKV_GL, (g + 1) * RWKV_GL)
        gram = _bdot_nt(jnp.concatenate([spread(q["bc"][:, sl]), spread(q["rc"][:, sl])], axis=0),
                        jnp.concatenate([_tile_rows(q["ac"][:, sl], RWKV_GROUP),
                                         _tile_rows(q["kc"][:, sl], RWKV_GROUP)], axis=0))
        br = jnp.concatenate([q["b0"][:, sl], q["r0"][:, sl]], axis=0)
        u0 = jnp.zeros((rows, RWKV_GL), F32)
        o0 = jnp.zeros((rows, RWKV_GL), F32)
        for s in range(cfg.nseq):
            res = _bdot_nt(br, s_scr[slot(u, s, g)])
            mine = seq_c == s
            u0 = u0 + _pick(cfg, mine, res[0:rows])
            o0 = o0 + _pick(cfg, mine, res[rows:2 * rows])
        yield
        x1 = jnp.where(strict, gram[0:n, 0:n], 0.0)
        x2 = jnp.where(strict, gram[0:n, n:2 * n], 0.0)
        x3 = jnp.where(incl, gram[n:2 * n, 0:n], 0.0)
        x4 = jnp.where(incl, gram[n:2 * n, n:2 * n], 0.0)
        vh = spread(q["v"][:, sl])
        xv = _bdot(jnp.concatenate([x2, x4], axis=0), vh)
        rhs = spread(u0) + xv[0:n]
        tinv = yield from _unit_lower_inverse(x1, lb)
        uh = _bdot(tinv, rhs)
        yield
        oh = _bdot(x3, uh) + xv[n:2 * n]
        outs[u][g] = o0 + _sum_row_blocks(oh, RWKV_GROUP)
        uv = jnp.concatenate([_sum_row_blocks(uh, RWKV_GROUP), q["v"][:, sl]], axis=0)
        akh = jnp.concatenate([q["ah"][:, sl], q["kh"][:, sl]], axis=0)
        for s in range(cfg.nseq):
            mine2 = jnp.concatenate([seq_c == s] * 2, axis=0)
            upd = _bdot_tn(_pick(cfg, mine2, uv), akh)
            s_scr[slot(u, s, g)] = (s_scr[slot(u, s, g)] * q["wc"][s * lb:s * lb + 1, sl]
                                    + jnp.where(blockdiag, upd, 0.0))
        yield

    preps = [prep(u) for u in range(cfg.units)]
    outs = [[None] * RWKV_NG for _ in range(cfg.units)]
    _interleave(group(u, g, preps[u][0], outs) for u in range(cfg.units) for g in range(RWKV_NG))
    for u in range(cfg.units):
        _, bonus, gate = preps[u]
        o = jnp.concatenate(outs[u], axis=1)
        mean = _head_sums(o, ones_pair) * (1.0 / N_A)
        oc = o - mean
        var = _head_sums(oc * oc, ones_pair) * (1.0 / N_A)
        o = oc * lax.rsqrt(var + RWKV_GN_EPS) * gn_g + gn_b
        o_ref[u] = (o + bonus) * gate

    @pl.when(c == cfg.nc - 1)
    def _():
        shout_ref[...] = pv_scr[...]
        for u in range(cfg.units):
            for s in range(cfg.nseq):
                for g in range(RWKV_NG):
                    sout_ref[u * cfg.nseq + s, :, g * RWKV_GL:(g + 1) * RWKV_GL] = _sum_row_blocks(
                        s_scr[slot(u, s, g)], RWKV_GROUP)


def rwkv(za, shift_prev, s0, mu, vecs, lora_w, ones_blk, i, cfg):
    params = [(mu, (None, 1, A_SHIFT_COLS), lambda b, c: (i, 0, 0)),
              (vecs, (None, RWKV_VEC_ROWS, C_A), lambda b, c: (i, 0, 0)),
              (lora_w, (None, 3, LORA_COLS, C_A), lambda b, c: (i, 0, 0, 0)),
              (ones_blk, (LANES, LANES), lambda b, c: (0, 0))]
    return _seq_call(_rwkv_kernel, cfg, "rwkv", [za], [shift_prev], [s0], params,
                     [C_A], [A_SHIFT_COLS], [(N_A, C_A)],
                     [pltpu.VMEM((cfg.units * cfg.nseq * RWKV_NG, RWKV_GL, RWKV_GL), F32),
                      pltpu.VMEM((cfg.units * cfg.hist, A_SHIFT_COLS), F32)], i)


def _trunk(x, cfg_rwkv, cfg_ret, cfg_gdn, st, p):
    has = st is not None
    h = x
    new = dict(shift=[], rwkv=[], ret=[], conv=[], gdn=[])
    for l in range(DEPTH):
        i = l // 2
        if l % 2 == 0:
            h = token_block(h, p, l, 0)
            za, zb = project(h, p["ab_in"], i, (A_SHIFT_COLS, B_COLS))
            o_a, sh, s_rw = rwkv(za, st["shift"] if has else None, st["rwkv"] if has else None,
                                 p["mu"], p["rwkv_vecs"], p["lora_w"], p["ones_blk"], i, cfg_rwkv)
            o_b, s_rt = retention(zb, st["ret"] if has else None, p["ret_g"], p["ret_b"], i, cfg_ret)
            mix = ([o_a, o_b], p["ab_out"], i)
            new["shift"].append(sh)
            new["rwkv"].append(s_rw)
            new["ret"].append(s_rt)
        else:
            h = token_block(h, p, l, 0)
            qkv, ba, zg = project(h, p["gdn_in"], i, (N_QKV_C, 2 * LANES, C_C))
            o, cv, s_gd = gdn(qkv, ba, zg, st["conv"] if has else None, st["gdn"] if has else None,
                              p["conv_w"], p["a_log"], p["dt_bias"], p["norm_g"], i, cfg_gdn)
            mix = ([o], p["gdn_out"], i)
            new["conv"].append(cv)
            new["gdn"].append(s_gd)
        h = token_block(h, p, l, 1, mix=mix)
    return h, {k: jnp.stack(v) for k, v in new.items()}


def _lane_pad(x, n):
    return jnp.pad(x, [(0, 0)] * (x.ndim - 1) + [(0, n - x.shape[-1])])


def _history_tiles(x):
    nl, b, r, n = x.shape
    return jnp.pad(x, ((0, 0), (0, 0), (SUBLANES - r, 0), (0, 0))).reshape(nl, b * SUBLANES, n)


def kernel(x_prompt, x_sample, state_rwkv_shift, state_rwkv, state_ret, state_gdn_conv, state_gdn, ln_g, ln_b, ffn_w_gate, ffn_w_up, ffn_w_down, ab_w_in, ab_w_out, rwkv_mu, rwkv_w0, rwkv_w2, rwkv_a0, rwkv_a2, rwkv_g2, rwkv_k_k, rwkv_k_a, rwkv_r_k, rwkv_gn_g, rwkv_gn_b, ret_gn_g, ret_gn_b, gdn_w_in, gdn_conv_w, gdn_A_log, gdn_dt_bias, gdn_norm_g, gdn_w_out):
    nb, seq = x_prompt.shape[0], x_prompt.shape[1]
    ndec, dec_seq = x_sample.shape[0], x_sample.shape[1]
    nl = N_AB_LAYERS

    lora_w = jnp.zeros((nl, 3, LORA_COLS, C_A), F32)
    lora_w = (lora_w.at[:, 0, 0:W_LORA].set(rwkv_w2).at[:, 1, W_LORA:W_LORA + A_LORA].set(rwkv_a2)
              .at[:, 2, W_LORA + A_LORA:].set(rwkv_g2))
    gdn_in = jnp.concatenate([gdn_w_in[:, :, :N_QKV_C],
                              _lane_pad(gdn_w_in[:, :, N_QKV_C:N_QKV_C + H_C], LANES),
                              _lane_pad(gdn_w_in[:, :, N_QKV_C + H_C:N_QKV_C + 2 * H_C], LANES),
                              gdn_w_in[:, :, N_QKV_C + 2 * H_C:]], axis=2)
    lane_head = np.arange(LANES) // N_A
    p = dict(
        wg=ffn_w_gate.astype(BF16), wu=ffn_w_up.astype(BF16), wd=ffn_w_down.astype(BF16),
        ln_g=ln_g.reshape(DEPTH * 3, 1, D_MODEL), ln_b=ln_b.reshape(DEPTH * 3, 1, D_MODEL),
        ab_in=ab_w_in.astype(BF16), ab_out=ab_w_out.astype(BF16),
        gdn_in=gdn_in.astype(BF16), gdn_out=gdn_w_out.astype(BF16),
        mu=rwkv_mu.reshape(nl, 1, A_SHIFT_COLS),
        rwkv_vecs=jnp.stack([rwkv_w0, rwkv_a0, rwkv_k_k, rwkv_k_a, rwkv_r_k.reshape(nl, C_A), rwkv_gn_g, rwkv_gn_b,
                             jnp.zeros_like(rwkv_w0)], axis=1),
        lora_w=lora_w.astype(BF16),
        ones_blk=jnp.asarray((lane_head[:, None] == lane_head[None, :]).astype(np.float32)),
        ret_g=ret_gn_g.reshape(nl, 1, C_B), ret_b=ret_gn_b.reshape(nl, 1, C_B),
        conv_w=gdn_conv_w, a_log=_lane_pad(gdn_A_log, LANES)[:, None, :], dt_bias=_lane_pad(gdn_dt_bias, LANES)[:, None, :],
        norm_g=gdn_norm_g[:, None, :],
    )

    hp, sp = _trunk(x_prompt.reshape(nb * seq, D_MODEL),
                    SeqCfg(1, 64, 64, seq // 64, 0, False, UNITS_DEEP),
                    SeqCfg(1, 128, 128, seq // 128, 0, False, UNITS),
                    SeqCfg(1, 64, 64, seq // 64, 0, False, UNITS_DEEP), None, p)
    cfg = SeqCfg(SEQ_PER_UNIT, SAMPLE_PAD, dec_seq, 1, PAST_LEN, True, UNITS)
    xs = jnp.pad(x_sample, ((0, 0), (0, SAMPLE_PAD - dec_seq), (0, 0))).reshape(ndec * SAMPLE_PAD, D_MODEL)
    st = dict(shift=_history_tiles(state_rwkv_shift[:, :, None, :]),
              rwkv=state_rwkv.transpose(0, 1, 3, 2, 4).reshape(nl, ndec, N_A, C_A),
              ret=state_ret,
              conv=_history_tiles(state_gdn_conv),
              gdn=state_gdn)
    hs, ss = _trunk(xs, cfg, cfg, cfg, st, p)

    def unpack(s, n):
        return (s["shift"].reshape(nl, n, SUBLANES, A_SHIFT_COLS)[:, :, SUBLANES - 1],
                s["rwkv"].reshape(nl, n, N_A, H_A, N_A).transpose(0, 1, 3, 2, 4),
                s["ret"],
                s["conv"].reshape(N_C_LAYERS, n, SUBLANES, N_QKV_C)[:, :, SUBLANES - (CONV_W - 1):],
                s["gdn"])

    y_prompt = hp.reshape(nb, seq, D_MODEL)
    y_sample = hs.reshape(ndec, SAMPLE_PAD, D_MODEL)[:, :dec_seq]
    return (y_prompt, y_sample) + unpack(sp, nb) + unpack(ss, ndec)
```

```python
import functools
import math

import numpy as np
import jax
import jax.numpy as jnp
from jax import lax
from jax.experimental import pallas as pl
from jax.experimental.pallas import tpu as pltpu

F32 = jnp.float32
BF16 = jnp.bfloat16

D_MODEL = 1024
DEPTH = 4
PAST_LEN = 16384
N_AB_LAYERS = (DEPTH + 1) // 2
N_C_LAYERS = DEPTH // 2
H_A, N_A = 8, 64
C_A = H_A * N_A
W_LORA, A_LORA, G_LORA = 64, 64, 128
LORA_COLS = W_LORA + A_LORA + G_LORA
A_SHIFT_COLS = 3 * C_A + LORA_COLS
RWKV_GN_EPS = 64e-5
H_B, DK_B, DV_B = 4, 128, 128
C_B = H_B * DV_B
ROPE_THETA = 10000.0
RET_GN_EPS = 1e-5
B_COLS = 2 * H_B * DK_B + 2 * C_B
AB_COLS = A_SHIFT_COLS + B_COLS
H_C, DK_C, DV_C = 8, 128, 128
C_C = H_C * DV_C
CONV_W = 4
N_QKV_C = 2 * H_C * DK_C + C_C
D_FF = 2816
LN_EPS = 1e-5
DEEPNORM_ALPHA = (2 * DEPTH) ** 0.25

V7X_VMEM_BYTES = 64 * 1024 * 1024
LANES = 128
SUBLANES = 8
VMEM_LIMIT = V7X_VMEM_BYTES * 7 // 8

SAMPLE_PAD = SUBLANES
SEQ_PER_UNIT = 8
UNITS = 2
UNITS_DEEP = 4
ROW_TILE = 512
RWKV_GROUP = 2
GDN_GROUP = 2


class SeqCfg:
    def __init__(self, nseq, lb, valid, nc, pos0, has_state, units):
        self.nseq, self.lb, self.valid, self.nc, self.pos0, self.has_state = nseq, lb, valid, nc, pos0, has_state
        self.units = units
        self.rows = nseq * lb
        self.shift = int(math.log2(lb))
        self.hist = nseq * SUBLANES
        assert 1 << self.shift == lb and lb >= SUBLANES and CONV_W - 1 <= valid <= lb


def _cparams(ngrid):
    return pltpu.CompilerParams(dimension_semantics=("arbitrary",) * ngrid, vmem_limit_bytes=VMEM_LIMIT)


def _bdot(a, b):
    return jnp.dot(a.astype(BF16), b.astype(BF16), preferred_element_type=F32)


def _bdot_nt(a, b):
    return lax.dot_general(a.astype(BF16), b.astype(BF16), (((1,), (1,)), ((), ())), preferred_element_type=F32)


def _bdot_tn(a, b):
    return lax.dot_general(a.astype(BF16), b.astype(BF16), (((0,), (0,)), ((), ())), preferred_element_type=F32)


def _split3(x):
    x1 = x.astype(BF16)
    r1 = x - x1.astype(F32)
    x2 = r1.astype(BF16)
    return x1, x2, (r1 - x2.astype(F32)).astype(BF16)


def _mask_dot(m, x):
    mb = m.astype(BF16)
    x1, x2, x3 = _split3(x)
    return (jnp.dot(mb, x1, preferred_element_type=F32) + jnp.dot(mb, x2, preferred_element_type=F32)
            + jnp.dot(mb, x3, preferred_element_type=F32))


def _sigmoid(x):
    return 1.0 / (1.0 + jnp.exp(-x))


def _silu(x):
    return x * _sigmoid(x)


def _softplus(x):
    return jnp.maximum(x, 0.0) + jnp.log(1.0 + jnp.exp(-jnp.abs(x)))


def _layer_norm(y, g, b):
    yc = y - jnp.mean(y, -1, keepdims=True)
    var = jnp.mean(yc * yc, -1, keepdims=True)
    return yc * lax.rsqrt(var + LN_EPS) * g + b


def _iota(shape, dim):
    return lax.broadcasted_iota(jnp.int32, shape, dim)


def _tile_rows(x, n):
    return jnp.concatenate([x] * n, axis=0)


def _sum_row_blocks(x, n):
    r = x.shape[0] // n
    out = x[0:r]
    for i in range(1, n):
        out = out + x[i * r:(i + 1) * r]
    return out


def _unit_lower_inverse(x, lb):
    n = x.shape[0]
    eye = (_iota((n, n), 0) == _iota((n, n), 1)).astype(F32)
    p = eye + x
    rounds = int(math.log2(lb)) - 1
    y = _bdot(x, x)
    yield
    for r in range(rounds):
        if r == rounds - 1:
            p = p + _bdot(p, y)
        else:
            z = _bdot(jnp.concatenate([p, y], axis=0), y)
            p = p + z[0:n]
            y = z[n:2 * n]
        yield
    return p


def _head_sums(x, ones_pair):
    rows = x.shape[0]
    nblk = x.shape[1] // LANES
    st = jnp.concatenate([x[:, j * LANES:(j + 1) * LANES] for j in range(nblk)], axis=0)
    s = _bdot(st, ones_pair)
    return jnp.concatenate([s[j * rows:(j + 1) * rows] for j in range(nblk)], axis=1)


def _interleave(gens):
    gens = list(gens)
    while gens:
        for g in list(gens):
            try:
                next(g)
            except StopIteration:
                gens.remove(g)


def _token_kernel(*refs, n_mix):
    it = iter(refs)
    x = next(it)[...]
    if n_mix:
        o_refs = [next(it) for _ in range(n_mix)]
        w_ref, g_ref, b_ref = next(it), next(it), next(it)
        acc = DEEPNORM_ALPHA * x
        off = 0
        for o_ref in o_refs:
            n = o_ref.shape[1]
            acc = acc + jnp.dot(o_ref[...].astype(BF16), w_ref[off:off + n, :], preferred_element_type=F32)
            off += n
        x = _layer_norm(acc, g_ref[...], b_ref[...])
    wg_ref, wu_ref, wd_ref, g_ref, b_ref, out_ref = (next(it) for _ in range(6))
    xb = x.astype(BF16)
    gate = jnp.dot(xb, wg_ref[...], preferred_element_type=F32)
    up = jnp.dot(xb, wu_ref[...], preferred_element_type=F32)
    act = (_silu(gate) * up).astype(BF16)
    f = jnp.dot(act, wd_ref[...], preferred_element_type=F32)
    out_ref[...] = _layer_norm(DEEPNORM_ALPHA * x + 0.5 * f, g_ref[...], b_ref[...])


def _resident(block_shape, index_map):
    return pl.BlockSpec(block_shape, index_map, pipeline_mode=pl.Buffered(1))


def token_block(h, p, l, j, mix=None):
    t = h.shape[0]
    row = lambda n: pl.BlockSpec((ROW_TILE, n), lambda r: (r, 0))
    ln = lambda k: pl.BlockSpec((None, 1, D_MODEL), lambda r: (3 * l + k, 0, 0))
    args, in_specs = [h], [row(D_MODEL)]
    if mix:
        outs, w_out, i = mix
        args += list(outs) + [w_out, p["ln_g"], p["ln_b"]]
        in_specs += [row(o.shape[1]) for o in outs] + [_resident((None, D_MODEL, D_MODEL), lambda r: (i, 0, 0)), ln(1), ln(1)]
    args += [p["wg"], p["wu"], p["wd"], p["ln_g"], p["ln_b"]]
    in_specs += [_resident((None, None, D_MODEL, D_FF), lambda r: (l, j, 0, 0)),
                 _resident((None, None, D_MODEL, D_FF), lambda r: (l, j, 0, 0)),
                 _resident((None, None, D_FF, D_MODEL), lambda r: (l, j, 0, 0)), ln(2 * j), ln(2 * j)]
    return pl.pallas_call(
        functools.partial(_token_kernel, n_mix=len(mix[0]) if mix else 0),
        grid=(t // ROW_TILE,),
        in_specs=in_specs,
        out_specs=row(D_MODEL),
        out_shape=jax.ShapeDtypeStruct((t, D_MODEL), F32),
        compiler_params=_cparams(1),
        name="token_block",
    )(*args)


def _proj_kernel(h_ref, w_ref, *o_refs):
    xb = h_ref[...].astype(BF16)
    off = 0
    for o_ref in o_refs:
        n = o_ref.shape[1]
        o_ref[...] = jnp.dot(xb, w_ref[:, off:off + n], preferred_element_type=F32)
        off += n


def project(h, w, i, widths):
    t = h.shape[0]
    ncols = w.shape[2]
    assert sum(widths) == ncols
    return pl.pallas_call(
        _proj_kernel,
        grid=(t // ROW_TILE,),
        in_specs=[pl.BlockSpec((ROW_TILE, D_MODEL), lambda r: (r, 0)),
                  _resident((None, D_MODEL, ncols), lambda r: (i, 0, 0))],
        out_specs=[pl.BlockSpec((ROW_TILE, n), lambda r: (r, 0)) for n in widths],
        out_shape=[jax.ShapeDtypeStruct((t, n), F32) for n in widths],
        compiler_params=_cparams(1),
        name="project",
    )(h, w)


def _row_info(cfg, shape, dim):
    r = _iota(shape, dim)
    return r >> cfg.shift, r & (cfg.lb - 1)


def _pick(cfg, mine, x):
    return x if cfg.nseq == 1 else jnp.where(mine, x, 0.0)


def _pair_masks(cfg, n):
    ri, ci = _iota((n, n), 0), _iota((n, n), 1)
    same = (ri >> cfg.shift) == (ci >> cfg.shift)
    ti, tj = ri & (cfg.lb - 1), ci & (cfg.lb - 1)
    return same & (tj <= ti), same & (tj < ti), same


def _shifted(cfg, x, hist, t_c, s):
    rolled = pltpu.roll(x, s, 0)
    if cfg.nseq == 1:
        head = jnp.where(t_c[0:SUBLANES] >= s, rolled[0:SUBLANES], pltpu.roll(hist, s, 0))
        return jnp.concatenate([head, rolled[SUBLANES:]], axis=0)
    assert cfg.hist == cfg.rows
    return jnp.where(t_c >= s, rolled, pltpu.roll(hist, cfg.rows - SUBLANES + s, 0))


def _new_history(cfg, x):
    rolled = x if cfg.lb == cfg.valid else pltpu.roll(x, cfg.lb - cfg.valid, 0)
    return jnp.concatenate([rolled[(j + 1) * cfg.lb - SUBLANES:(j + 1) * cfg.lb] for j in range(cfg.nseq)], axis=0)


def _seq_call(kernel, cfg, name, rows_in, hist_in, state_in, params, rows_out, hist_out, state_out, scratch, i):
    t = rows_in[0].shape[0]
    nbt = t // (cfg.rows * cfg.nc)
    assert nbt % cfg.units == 0
    grid = (nbt // cfg.units, cfg.nc)

    def zeros(n):
        return (0,) * n

    row_spec = lambda n: pl.BlockSpec((cfg.units, None, cfg.rows, n), lambda b, c: (b, c, 0, 0))
    args = [a.reshape(nbt, cfg.nc, cfg.rows, a.shape[1]) for a in rows_in]
    in_specs = [row_spec(a.shape[1]) for a in rows_in]
    if cfg.has_state:
        for a in hist_in:
            args.append(a)
            in_specs.append(pl.BlockSpec((None, cfg.units * cfg.hist, a.shape[2]), lambda b, c: (i, b, 0)))
        for a in state_in:
            nd = a.ndim - 2
            args.append(a)
            in_specs.append(pl.BlockSpec((None, cfg.units * cfg.nseq) + a.shape[2:],
                                         lambda b, c, nd=nd: (i, b) + zeros(nd)))
    for a, bs, im in params:
        args.append(a)
        in_specs.append(pl.BlockSpec(bs, im))
    out_specs = [row_spec(n) for n in rows_out]
    out_shape = [jax.ShapeDtypeStruct((nbt, cfg.nc, cfg.rows, n), F32) for n in rows_out]
    for n in hist_out:
        out_specs.append(pl.BlockSpec((cfg.units * cfg.hist, n), lambda b, c: (b, 0)))
        out_shape.append(jax.ShapeDtypeStruct((nbt * cfg.hist, n), F32))
    for shp in state_out:
        out_specs.append(pl.BlockSpec((cfg.units * cfg.nseq,) + shp, lambda b, c, nd=len(shp): (b,) + zeros(nd)))
        out_shape.append(jax.ShapeDtypeStruct((nbt * cfg.nseq,) + shp, F32))
    outs = pl.pallas_call(
        functools.partial(kernel, cfg=cfg),
        grid=grid, in_specs=in_specs, out_specs=out_specs, out_shape=out_shape,
        scratch_shapes=scratch, compiler_params=_cparams(2), name=name,
    )(*args)
    return [o.reshape(t, o.shape[-1]) for o in outs[:len(rows_out)]] + list(outs[len(rows_out):])


def _ret_kernel(*refs, cfg):
    if cfg.has_state:
        zb_ref, s0_ref, g_ref, b_ref, o_ref, sout_ref, s_scr = refs
    else:
        zb_ref, g_ref, b_ref, o_ref, sout_ref, s_scr = refs
    c = pl.program_id(1)
    rows = cfg.rows

    @pl.when(c == 0)
    def _():
        s_scr[...] = s0_ref[...] if cfg.has_state else jnp.zeros(s_scr.shape, F32)

    seq_c, t_c = _row_info(cfg, (rows, 1), 0)
    live = t_c < cfg.valid
    incl, _, _ = _pair_masks(cfg, rows)
    ti = (_iota((rows, rows), 0) & (cfg.lb - 1))
    tj = (_iota((rows, rows), 1) & (cfg.lb - 1))
    steps_i = jnp.minimum(ti + 1, cfg.valid).astype(F32)
    steps_j = jnp.minimum(tj + 1, cfg.valid).astype(F32)
    steps_c = jnp.minimum(t_c + 1, cfg.valid).astype(F32)

    lane = _iota((rows, LANES), 1)
    half = DK_B // 2
    inv_freq = 1.0 / jnp.power(ROPE_THETA, (lane & (half - 1)).astype(F32) / (half - 1.0))
    pos = (cfg.pos0 + c * cfg.lb + t_c).astype(F32)
    ang = pos * inv_freq
    cos = jnp.cos(ang)
    sin = jnp.sin(ang)
    sin = jnp.where(lane < half, -sin, sin)

    def rot(x):
        return x * cos + pltpu.roll(x, half, 1) * sin

    def head(u, h):
        sl = slice(h * DK_B, (h + 1) * DK_B)
        q = rot(zb_ref[u, :, sl])
        k = rot(zb_ref[u, :, H_B * DK_B + h * DK_B:H_B * DK_B + (h + 1) * DK_B]) * DK_B ** -0.5
        v = zb_ref[u, :, 2 * H_B * DK_B + h * DV_B:2 * H_B * DK_B + (h + 1) * DV_B]
        gate = zb_ref[u, :, 2 * H_B * DK_B + C_B + h * DV_B:2 * H_B * DK_B + C_B + (h + 1) * DV_B]
        k = jnp.where(live, k, 0.0)
        v = jnp.where(live, v, 0.0)
        lam = math.log(1.0 - 2.0 ** (-5.0 - h))
        dmat = jnp.where(incl, jnp.exp(lam * (steps_i - steps_j)), 0.0)
        sc = _bdot_nt(q, k) * dmat
        yield
        o = _bdot(sc, v)
        kz = k * jnp.exp(lam * (cfg.valid - steps_c))
        xi = jnp.exp(lam * steps_c)
        for s in range(cfg.nseq):
            mine = seq_c == s
            st = s_scr[u * cfg.nseq + s, h]
            o = o + _pick(cfg, mine, _bdot(q, st) * xi)
            s_scr[u * cfg.nseq + s, h] = math.exp(lam * cfg.valid) * st + _bdot_tn(_pick(cfg, mine, kz), v)
        yield
        oc = o - jnp.mean(o, -1, keepdims=True)
        var = jnp.mean(oc * oc, -1, keepdims=True)
        o = oc * lax.rsqrt(var + RET_GN_EPS) * g_ref[:, sl] + b_ref[:, sl]
        o_ref[u, :, sl] = o * _silu(gate)

    _interleave(head(u, h) for u in range(cfg.units) for h in range(H_B))

    @pl.when(c == cfg.nc - 1)
    def _():
        sout_ref[...] = s_scr[...]


def retention(zb, s0, gn_g, gn_b, i, cfg):
    par = ((None, 1, C_B), lambda b, c: (i, 0, 0))
    return _seq_call(_ret_kernel, cfg, "retention", [zb], [], [s0], [(gn_g,) + par, (gn_b,) + par],
                     [C_B], [], [(H_B, DK_B, DV_B)],
                     [pltpu.VMEM((cfg.units * cfg.nseq, H_B, DK_B, DV_B), F32)], i)


def _gdn_kernel(*refs, cfg):
    if cfg.has_state:
        (qkv_ref, ba_ref, zg_ref, pv_ref, s0_ref, cw_ref, alog_ref, dtb_ref, ng_ref,
         o_ref, cvout_ref, sout_ref, s_scr, pv_scr) = refs
    else:
        (qkv_ref, ba_ref, zg_ref, cw_ref, alog_ref, dtb_ref, ng_ref,
         o_ref, cvout_ref, sout_ref, s_scr, pv_scr) = refs
    c = pl.program_id(1)
    rows, lb, valid, nhist = cfg.rows, cfg.lb, cfg.valid, cfg.hist

    @pl.when(c == 0)
    def _():
        s_scr[...] = s0_ref[...] if cfg.has_state else jnp.zeros(s_scr.shape, F32)
        for u in range(cfg.units):
            pv_scr[u, 0:nhist] = (pv_ref[u * nhist:(u + 1) * nhist] if cfg.has_state
                                  else jnp.zeros((nhist, N_QKV_C), F32))

    seq_c, t_c = _row_info(cfg, (rows, 1), 0)
    live = t_c < valid
    incl_r, _, same_r = _pair_masks(cfg, rows)
    n = GDN_GROUP * rows
    incl, strict, _ = _pair_masks(cfg, n)

    def prep(u):
        x = qkv_ref[u]
        conv = cw_ref[CONV_W - 1:CONV_W, :] * x
        hist = pv_scr[u, 0:nhist]
        for s in range(1, CONV_W):
            conv = conv + cw_ref[CONV_W - 1 - s:CONV_W - s, :] * _shifted(cfg, x, hist, t_c, s)
        pv_scr[u, 0:nhist] = _new_history(cfg, x)
        qkv = _silu(conv)
        ba = ba_ref[u]
        beta_all = jnp.where(live, _sigmoid(ba[:, 0:LANES]), 0.0)
        g_all = jnp.where(live, -jnp.exp(alog_ref[...]) * _softplus(ba[:, LANES:2 * LANES] + dtb_ref[...]), 0.0)
        return qkv, beta_all, _mask_dot(incl_r, g_all), _mask_dot(same_r, g_all)

    def group(u, gi, qkv, beta_all, gc_all, gl_all):
        heads = range(gi * GDN_GROUP, (gi + 1) * GDN_GROUP)

        def stack(arr, base):
            return jnp.concatenate([arr[:, base + DK_C * h:base + DK_C * (h + 1)] for h in heads], axis=0)

        def stack_col(arr):
            return jnp.concatenate([arr[:, h:h + 1] for h in heads], axis=0)

        def unit(a):
            return a * lax.rsqrt(jnp.maximum(jnp.sum(a * a, -1, keepdims=True), 1e-12))

        qs = unit(stack(qkv, 0)) * DK_C ** -0.5
        ks = unit(stack(qkv, H_C * DK_C))
        vs = stack(qkv, 2 * H_C * DK_C)
        bs, gcs, gls = stack_col(beta_all), stack_col(gc_all), stack_col(gl_all)
        diff = gcs - gcs.T
        dmat = jnp.where(incl, jnp.exp(jnp.where(incl, diff, 0.0)), 0.0)
        kk = _bdot_nt(ks, ks)
        qk = _bdot_nt(qs, ks) * dmat
        yield
        a = jnp.where(strict, bs * kk * dmat, 0.0)
        tinv = yield from _unit_lower_inverse(-a, lb)
        egc = jnp.exp(gcs)
        uw = _bdot(tinv, jnp.concatenate([vs * bs, ks * (bs * egc)], axis=1))
        u_, w = uw[:, 0:DV_C], uw[:, DV_C:DV_C + DK_C]
        yield
        qg = qs * egc
        kd = ks * jnp.exp(gls - gcs)
        egl = jnp.exp(gls)
        v_new, o_state = [], []
        for hh, h in enumerate(heads):
            rs = slice(hh * rows, (hh + 1) * rows)
            wq = jnp.concatenate([w[rs], qg[rs]], axis=0)
            ws = jnp.zeros((rows, DV_C), F32)
            os_ = jnp.zeros((rows, DV_C), F32)
            for s in range(cfg.nseq):
                res = _bdot(wq, s_scr[u * cfg.nseq + s, h])
                mine = seq_c == s
                ws = ws + _pick(cfg, mine, res[0:rows])
                os_ = os_ + _pick(cfg, mine, res[rows:2 * rows])
            v_new.append(u_[rs] - ws)
            o_state.append(os_)
        yield
        v_new_s = jnp.concatenate(v_new, axis=0)
        o_s = jnp.concatenate(o_state, axis=0) + _bdot(qk, v_new_s)
        for hh, h in enumerate(heads):
            rs = slice(hh * rows, (hh + 1) * rows)
            for s in range(cfg.nseq):
                mine = seq_c == s
                r0 = hh * rows + s * lb
                s_scr[u * cfg.nseq + s, h] = (egl[r0:r0 + 1, :] * s_scr[u * cfg.nseq + s, h]
                                              + _bdot_tn(_pick(cfg, mine, kd[rs]), v_new[hh]))
        yield
        for hh, h in enumerate(heads):
            o = o_s[hh * rows:(hh + 1) * rows]
            o = o * lax.rsqrt(jnp.mean(o * o, -1, keepdims=True) + 1e-6) * ng_ref[...]
            o_ref[u, :, h * DV_C:(h + 1) * DV_C] = o * _silu(zg_ref[u, :, h * DV_C:(h + 1) * DV_C])

    preps = [prep(u) for u in range(cfg.units)]
    _interleave(group(u, gi, *preps[u]) for u in range(cfg.units) for gi in range(H_C // GDN_GROUP))

    @pl.when(c == cfg.nc - 1)
    def _():
        for u in range(cfg.units):
            cvout_ref[u * nhist:(u + 1) * nhist] = pv_scr[u, 0:nhist]
        sout_ref[...] = s_scr[...]


def gdn(qkv, ba, zg, conv_prev, s0, conv_w, a_log, dt_bias, norm_g, i, cfg):
    par = lambda n: ((None, 1, n), lambda b, c: (i, 0, 0))
    params = [(conv_w, (None, CONV_W, N_QKV_C), lambda b, c: (i, 0, 0)), (a_log,) + par(LANES),
              (dt_bias,) + par(LANES), (norm_g,) + par(DV_C)]
    return _seq_call(_gdn_kernel, cfg, "gdn", [qkv, ba, zg], [conv_prev], [s0], params,
                     [C_C], [N_QKV_C], [(H_C, DK_C, DV_C)],
                     [pltpu.VMEM((cfg.units * cfg.nseq, H_C, DK_C, DV_C), F32),
                      pltpu.VMEM((cfg.units, cfg.hist, N_QKV_C), F32)], i)


RWKV_VEC_ROWS = 8
RWKV_GL = RWKV_GROUP * N_A
RWKV_NG = H_A // RWKV_GROUP


def _rwkv_kernel(*refs, cfg):
    if cfg.has_state:
        (za_ref, pv_ref, s0_ref, mu_ref, vec_ref, lw_ref, ones_ref,
         o_ref, shout_ref, sout_ref, s_scr, pv_scr) = refs
    else:
        (za_ref, mu_ref, vec_ref, lw_ref, ones_ref, o_ref, shout_ref, sout_ref, s_scr, pv_scr) = refs
    c = pl.program_id(1)
    rows, lb, valid = cfg.rows, cfg.lb, cfg.valid
    n = RWKV_GROUP * rows
    head_shift = int(math.log2(N_A))
    row_shift = int(math.log2(rows))
    assert 1 << row_shift == rows

    bi, bj = _iota((RWKV_GL, RWKV_GL), 0), _iota((RWKV_GL, RWKV_GL), 1)
    blockdiag = (bi >> head_shift) == (bj >> head_shift)

    def slot(u, s, g):
        return (u * cfg.nseq + s) * RWKV_NG + g

    @pl.when(c == 0)
    def _():
        pv_scr[...] = pv_ref[...] if cfg.has_state else jnp.zeros(pv_scr.shape, F32)
        for u in range(cfg.units):
            for s in range(cfg.nseq):
                for g in range(RWKV_NG):
                    if cfg.has_state:
                        sc = s0_ref[u * cfg.nseq + s, :, g * RWKV_GL:(g + 1) * RWKV_GL]
                        s_scr[slot(u, s, g)] = jnp.where(blockdiag, _tile_rows(sc, RWKV_GROUP), 0.0)
                    else:
                        s_scr[slot(u, s, g)] = jnp.zeros((RWKV_GL, RWKV_GL), F32)

    seq_c, t_c = _row_info(cfg, (rows, 1), 0)
    live = t_c < valid
    incl_r, _, same_r = _pair_masks(cfg, rows)
    tj = _iota((rows, rows), 1) & (lb - 1)
    sum_masks = jnp.concatenate([incl_r.astype(BF16), (same_r & (tj < lb // 2)).astype(BF16), same_r.astype(BF16)],
                                axis=0)
    hi, hl = _iota((n, RWKV_GL), 0), _iota((n, RWKV_GL), 1)
    headmask = (hi >> row_shift) == (hl >> head_shift)
    incl, strict, _ = _pair_masks(cfg, n)
    spread = lambda x: jnp.where(headmask, _tile_rows(x, RWKV_GROUP), 0.0)
    w0, a0, k_k, k_a, r_k, gn_g, gn_b = (vec_ref[j:j + 1, :] for j in range(7))
    ones_pair = ones_ref[...]

    def prep(u):
        za = za_ref[u]
        za_prev = _shifted(cfg, za, pv_scr[u * cfg.hist:(u + 1) * cfg.hist], t_c, 1)
        pv_scr[u * cfg.hist:(u + 1) * cfg.hist] = _new_history(cfg, za)
        zs = za + (za_prev - za) * mu_ref[...]
        r = zs[:, 0:C_A]
        k = zs[:, C_A:2 * C_A]
        v = zs[:, 2 * C_A:3 * C_A]
        lo = zs[:, 3 * C_A:A_SHIFT_COLS]
        ll = _iota((rows, LORA_COLS), 1)
        lo = jnp.where(ll < W_LORA, jnp.tanh(lo), jnp.where(ll < W_LORA + A_LORA, lo, _sigmoid(lo)))
        w_log = -_softplus(-(w0 + _bdot(lo, lw_ref[0]))) - 0.5
        logw = -jnp.exp(w_log)
        a = _sigmoid(a0 + _bdot(lo, lw_ref[1]))
        gate = _bdot(lo, lw_ref[2])
        kk = k * k_k
        k = k * (1.0 + (a - 1.0) * k_a)
        sums = _head_sums(jnp.concatenate([kk * kk, r * k * r_k], axis=0), ones_pair)
        kk = kk * lax.rsqrt(jnp.maximum(sums[0:rows], 1e-12))
        bonus = sums[rows:2 * rows] * v
        logw = jnp.where(live, logw, 0.0)
        kk = jnp.where(live, kk, 0.0)
        k = jnp.where(live, k, 0.0)
        v = jnp.where(live, v, 0.0)
        ka = kk * a
        sums = _mask_dot(sum_masks, logw)
        cl, cm, ct = sums[0:rows], sums[rows:2 * rows], sums[2 * rows:3 * rows]
        clp = cl - logw
        e_mid = jnp.exp(cm - cl)
        e_end = jnp.exp(ct - cl)
        vecs = dict(
            b0=kk * jnp.exp(clp), r0=r * jnp.exp(cl),
            bc=kk * jnp.exp(clp - cm), rc=r * jnp.exp(cl - cm),
            ac=-ka * e_mid, kc=k * e_mid,
            ah=-ka * e_end, kh=k * e_end, wc=jnp.exp(ct), v=v)
        return vecs, bonus, gate

    def group(u, g, q, outs):
        sl = slice(g * RWKV_GL, (g + 1) * RWKV_GL)
        gram = _bdot_nt(jnp.concatenate([spread(q["bc"][:, sl]), spread(q["rc"][:, sl])], axis=0),
                        jnp.concatenate([_tile_rows(q["ac"][:, sl], RWKV_GROUP),
                                         _tile_rows(q["kc"][:, sl], RWKV_GROUP)], axis=0))
        br = jnp.concatenate([q["b0"][:, sl], q["r0"][:, sl]], axis=0)
        u0 = jnp.zeros((rows, RWKV_GL), F32)
        o0 = jnp.zeros((rows, RWKV_GL), F32)
        for s in range(cfg.nseq):
            res = _bdot_nt(br, s_scr[slot(u, s, g)])
            mine = seq_c == s
            u0 = u0 + _pick(cfg, mine, res[0:rows])
            o0 = o0 + _pick(cfg, mine, res[rows:2 * rows])
        yield
        x1 = jnp.where(strict, gram[0:n, 0:n], 0.0)
        x2 = jnp.where(strict, gram[0:n, n:2 * n], 0.0)
        x3 = jnp.where(incl, gram[n:2 * n, 0:n], 0.0)
        x4 = jnp.where(incl, gram[n:2 * n, n:2 * n], 0.0)
        vh = spread(q["v"][:, sl])
        xv = _bdot(jnp.concatenate([x2, x4], axis=0), vh)
        rhs = spread(u0) + xv[0:n]
        tinv = yield from _unit_lower_inverse(x1, lb)
        uh = _bdot(tinv, rhs)
        yield
        oh = _bdot(x3, uh) + xv[n:2 * n]
        outs[u][g] = o0 + _sum_row_blocks(oh, RWKV_GROUP)
        uv = jnp.concatenate([_sum_row_blocks(uh, RWKV_GROUP), q["v"][:, sl]], axis=0)
        akh = jnp.concatenate([q["ah"][:, sl], q["kh"][:, sl]], axis=0)
        for s in range(cfg.nseq):
            mine2 = jnp.concatenate([seq_c == s] * 2, axis=0)
            upd = _bdot_tn(_pick(cfg, mine2, uv), akh)
            s_scr[slot(u, s, g)] = (s_scr[slot(u, s, g)] * q["wc"][s * lb:s * lb + 1, sl]
                                    + jnp.where(blockdiag, upd, 0.0))
        yield

    preps = [prep(u) for u in range(cfg.units)]
    outs = [[None] * RWKV_NG for _ in range(cfg.units)]
    _interleave(group(u, g, preps[u][0], outs) for u in range(cfg.units) for g in range(RWKV_NG))
    for u in range(cfg.units):
        _, bonus, gate = preps[u]
        o = jnp.concatenate(outs[u], axis=1)
        mean = _head_sums(o, ones_pair) * (1.0 / N_A)
        oc = o - mean
        var = _head_sums(oc * oc, ones_pair) * (1.0 / N_A)
        o = oc * lax.rsqrt(var + RWKV_GN_EPS) * gn_g + gn_b
        o_ref[u] = (o + bonus) * gate

    @pl.when(c == cfg.nc - 1)
    def _():
        shout_ref[...] = pv_scr[...]
        for u in range(cfg.units):
            for s in range(cfg.nseq):
                for g in range(RWKV_NG):
                    sout_ref[u * cfg.nseq + s, :, g * RWKV_GL:(g + 1) * RWKV_GL] = _sum_row_blocks(
                        s_scr[slot(u, s, g)], RWKV_GROUP)


def rwkv(za, shift_prev, s0, mu, vecs, lora_w, ones_blk, i, cfg):
    params = [(mu, (None, 1, A_SHIFT_COLS), lambda b, c: (i, 0, 0)),
              (vecs, (None, RWKV_VEC_ROWS, C_A), lambda b, c: (i, 0, 0)),
              (lora_w, (None, 3, LORA_COLS, C_A), lambda b, c: (i, 0, 0, 0)),
              (ones_blk, (LANES, LANES), lambda b, c: (0, 0))]
    return _seq_call(_rwkv_kernel, cfg, "rwkv", [za], [shift_prev], [s0], params,
                     [C_A], [A_SHIFT_COLS], [(N_A, C_A)],
                     [pltpu.VMEM((cfg.units * cfg.nseq * RWKV_NG, RWKV_GL, RWKV_GL), F32),
                      pltpu.VMEM((cfg.units * cfg.hist, A_SHIFT_COLS), F32)], i)


def _trunk(x, cfg_rwkv, cfg_ret, cfg_gdn, st, p):
    has = st is not None
    h = x
    new = dict(shift=[], rwkv=[], ret=[], conv=[], gdn=[])
    for l in range(DEPTH):
        i = l // 2
        if l % 2 == 0:
            h = token_block(h, p, l, 0)
            za, zb = project(h, p["ab_in"], i, (A_SHIFT_COLS, B_COLS))
            o_a, sh, s_rw = rwkv(za, st["shift"] if has else None, st["rwkv"] if has else None,
                                 p["mu"], p["rwkv_vecs"], p["lora_w"], p["ones_blk"], i, cfg_rwkv)
            o_b, s_rt = retention(zb, st["ret"] if has else None, p["ret_g"], p["ret_b"], i, cfg_ret)
            mix = ([o_a, o_b], p["ab_out"], i)
            new["shift"].append(sh)
            new["rwkv"].append(s_rw)
            new["ret"].append(s_rt)
        else:
            h = token_block(h, p, l, 0)
            qkv, ba, zg = project(h, p["gdn_in"], i, (N_QKV_C, 2 * LANES, C_C))
            o, cv, s_gd = gdn(qkv, ba, zg, st["conv"] if has else None, st["gdn"] if has else None,
                              p["conv_w"], p["a_log"], p["dt_bias"], p["norm_g"], i, cfg_gdn)
            mix = ([o], p["gdn_out"], i)
            new["conv"].append(cv)
            new["gdn"].append(s_gd)
        h = token_block(h, p, l, 1, mix=mix)
    return h, {k: jnp.stack(v) for k, v in new.items()}


def _lane_pad(x, n):
    return jnp.pad(x, [(0, 0)] * (x.ndim - 1) + [(0, n - x.shape[-1])])


def _history_tiles(x):
    nl, b, r, n = x.shape
    return jnp.pad(x, ((0, 0), (0, 0), (SUBLANES - r, 0), (0, 0))).reshape(nl, b * SUBLANES, n)


def kernel(x_prompt, x_sample, state_rwkv_shift, state_rwkv, state_ret, state_gdn_conv, state_gdn, ln_g, ln_b, ffn_w_gate, ffn_w_up, ffn_w_down, ab_w_in, ab_w_out, rwkv_mu, rwkv_w0, rwkv_w2, rwkv_a0, rwkv_a2, rwkv_g2, rwkv_k_k, rwkv_k_a, rwkv_r_k, rwkv_gn_g, rwkv_gn_b, ret_gn_g, ret_gn_b, gdn_w_in, gdn_conv_w, gdn_A_log, gdn_dt_bias, gdn_norm_g, gdn_w_out):
    nb, seq = x_prompt.shape[0], x_prompt.shape[1]
    ndec, dec_seq = x_sample.shape[0], x_sample.shape[1]
    nl = N_AB_LAYERS

    lora_w = jnp.zeros((nl, 3, LORA_COLS, C_A), F32)
    lora_w = (lora_w.at[:, 0, 0:W_LORA].set(rwkv_w2).at[:, 1, W_LORA:W_LORA + A_LORA].set(rwkv_a2)
              .at[:, 2, W_LORA + A_LORA:].set(rwkv_g2))
    gdn_in = jnp.concatenate([gdn_w_in[:, :, :N_QKV_C],
                              _lane_pad(gdn_w_in[:, :, N_QKV_C:N_QKV_C + H_C], LANES),
                              _lane_pad(gdn_w_in[:, :, N_QKV_C + H_C:N_QKV_C + 2 * H_C], LANES),
                              gdn_w_in[:, :, N_QKV_C + 2 * H_C:]], axis=2)
    lane_head = np.arange(LANES) // N_A
    p = dict(
        wg=ffn_w_gate.astype(BF16), wu=ffn_w_up.astype(BF16), wd=ffn_w_down.astype(BF16),
        ln_g=ln_g.reshape(DEPTH * 3, 1, D_MODEL), ln_b=ln_b.reshape(DEPTH * 3, 1, D_MODEL),
        ab_in=ab_w_in.astype(BF16), ab_out=ab_w_out.astype(BF16),
        gdn_in=gdn_in.astype(BF16), gdn_out=gdn_w_out.astype(BF16),
        mu=rwkv_mu.reshape(nl, 1, A_SHIFT_COLS),
        rwkv_vecs=jnp.stack([rwkv_w0, rwkv_a0, rwkv_k_k, rwkv_k_a, rwkv_r_k.reshape(nl, C_A), rwkv_gn_g, rwkv_gn_b,
                             jnp.zeros_like(rwkv_w0)], axis=1),
        lora_w=lora_w.astype(BF16),
        ones_blk=jnp.asarray((lane_head[:, None] == lane_head[None, :]).astype(np.float32)),
        ret_g=ret_gn_g.reshape(nl, 1, C_B), ret_b=ret_gn_b.reshape(nl, 1, C_B),
        conv_w=gdn_conv_w, a_log=_lane_pad(gdn_A_log, LANES)[:, None, :], dt_bias=_lane_pad(gdn_dt_bias, LANES)[:, None, :],
        norm_g=gdn_norm_g[:, None, :],
    )

    hp, sp = _trunk(x_prompt.reshape(nb * seq, D_MODEL),
                    SeqCfg(1, 64, 64, seq // 64, 0, False, UNITS_DEEP),
                    SeqCfg(1, 128, 128, seq // 128, 0, False, UNITS),
                    SeqCfg(1, 64, 64, seq // 64, 0, False, UNITS_DEEP), None, p)
    cfg = SeqCfg(SEQ_PER_UNIT, SAMPLE_PAD, dec_seq, 1, PAST_LEN, True, UNITS)
    xs = jnp.pad(x_sample, ((0, 0), (0, SAMPLE_PAD - dec_seq), (0, 0))).reshape(ndec * SAMPLE_PAD, D_MODEL)
    st = dict(shift=_history_tiles(state_rwkv_shift[:, :, None, :]),
              rwkv=state_rwkv.transpose(0, 1, 3, 2, 4).reshape(nl, ndec, N_A, C_A),
              ret=state_ret,
              conv=_history_tiles(state_gdn_conv),
              gdn=state_gdn)
    hs, ss = _trunk(xs, cfg, cfg, cfg, st, p)

    def unpack(s, n):
        return (s["shift"].reshape(nl, n, SUBLANES, A_SHIFT_COLS)[:, :, SUBLANES - 1],
                s["rwkv"].reshape(nl, n, N_A, H_A, N_A).transpose(0, 1, 3, 2, 4),
                s["ret"],
                s["conv"].reshape(N_C_LAYERS, n, SUBLANES, N_QKV_C)[:, :, SUBLANES - (CONV_W - 1):],
                s["gdn"])

    y_prompt = hp.reshape(nb, seq, D_MODEL)
    y_sample = hs.reshape(ndec, SAMPLE_PAD, D_MODEL)[:, :dec_seq]
    return (y_prompt, y_sample) + unpack(sp, nb) + unpack(ss, ndec)
```

```python
import functools
import math

import numpy as np
import jax
import jax.numpy as jnp
from jax import lax
from jax.experimental import pallas as pl
from jax.experimental.pallas import tpu as pltpu

F32 = jnp.float32
BF16 = jnp.bfloat16

D_MODEL = 1024
DEPTH = 4
PAST_LEN = 16384
N_AB_LAYERS = (DEPTH + 1) // 2
N_C_LAYERS = DEPTH // 2
H_A, N_A = 8, 64
C_A = H_A * N_A
W_LORA, A_LORA, G_LORA = 64, 64, 128
LORA_COLS = W_LORA + A_LORA + G_LORA
A_SHIFT_COLS = 3 * C_A + LORA_COLS
RWKV_GN_EPS = 64e-5
H_B, DK_B, DV_B = 4, 128, 128
C_B = H_B * DV_B
ROPE_THETA = 10000.0
RET_GN_EPS = 1e-5
B_COLS = 2 * H_B * DK_B + 2 * C_B
AB_COLS = A_SHIFT_COLS + B_COLS
H_C, DK_C, DV_C = 8, 128, 128
C_C = H_C * DV_C
CONV_W = 4
N_QKV_C = 2 * H_C * DK_C + C_C
D_FF = 2816
LN_EPS = 1e-5
DEEPNORM_ALPHA = (2 * DEPTH) ** 0.25

V7X_VMEM_BYTES = 64 * 1024 * 1024
LANES = 128
SUBLANES = 8
VMEM_LIMIT = V7X_VMEM_BYTES * 7 // 8

SAMPLE_PAD = SUBLANES
SEQ_PER_UNIT = 8
UNITS = 2
UNITS_DEEP = 4
ROW_TILE = 512
RWKV_GROUP = 2
GDN_GROUP = 2


class SeqCfg:
    def __init__(self, nseq, lb, valid, nc, pos0, has_state, units):
        self.nseq, self.lb, self.valid, self.nc, self.pos0, self.has_state = nseq, lb, valid, nc, pos0, has_state
        self.units = units
        self.rows = nseq * lb
        self.shift = int(math.log2(lb))
        self.hist = nseq * SUBLANES
        assert 1 << self.shift == lb and lb >= SUBLANES and CONV_W - 1 <= valid <= lb


def _cparams(ngrid):
    return pltpu.CompilerParams(dimension_semantics=("arbitrary",) * ngrid, vmem_limit_bytes=VMEM_LIMIT)


def _bdot(a, b):
    return jnp.dot(a.astype(BF16), b.astype(BF16), preferred_element_type=F32)


def _bdot_nt(a, b):
    return lax.dot_general(a.astype(BF16), b.astype(BF16), (((1,), (1,)), ((), ())), preferred_element_type=F32)


def _bdot_tn(a, b):
    return lax.dot_general(a.astype(BF16), b.astype(BF16), (((0,), (0,)), ((), ())), preferred_element_type=F32)


def _split3(x):
    x1 = x.astype(BF16)
    r1 = x - x1.astype(F32)
    x2 = r1.astype(BF16)
    return x1, x2, (r1 - x2.astype(F32)).astype(BF16)


def _mask_dot(m, x):
    mb = m.astype(BF16)
    x1, x2, x3 = _split3(x)
    return (jnp.dot(mb, x1, preferred_element_type=F32) + jnp.dot(mb, x2, preferred_element_type=F32)
            + jnp.dot(mb, x3, preferred_element_type=F32))


def _sigmoid(x):
    return 1.0 / (1.0 + jnp.exp(-x))


def _silu(x):
    return x * _sigmoid(x)


def _softplus(x):
    return jnp.maximum(x, 0.0) + jnp.log(1.0 + jnp.exp(-jnp.abs(x)))


def _layer_norm(y, g, b):
    yc = y - jnp.mean(y, -1, keepdims=True)
    var = jnp.mean(yc * yc, -1, keepdims=True)
    return yc * lax.rsqrt(var + LN_EPS) * g + b


def _iota(shape, dim):
    return lax.broadcasted_iota(jnp.int32, shape, dim)


def _tile_rows(x, n):
    return jnp.concatenate([x] * n, axis=0)


def _sum_row_blocks(x, n):
    r = x.shape[0] // n
    out = x[0:r]
    for i in range(1, n):
        out = out + x[i * r:(i + 1) * r]
    return out


def _unit_lower_inverse(x, lb):
    n = x.shape[0]
    eye = (_iota((n, n), 0) == _iota((n, n), 1)).astype(F32)
    p = eye + x
    rounds = int(math.log2(lb)) - 1
    y = _bdot(x, x)
    yield
    for r in range(rounds):
        if r == rounds - 1:
            p = p + _bdot(p, y)
        else:
            z = _bdot(jnp.concatenate([p, y], axis=0), y)
            p = p + z[0:n]
            y = z[n:2 * n]
        yield
    return p


def _head_sums(x, ones_pair):
    rows = x.shape[0]
    nblk = x.shape[1] // LANES
    st = jnp.concatenate([x[:, j * LANES:(j + 1) * LANES] for j in range(nblk)], axis=0)
    s = _bdot(st, ones_pair)
    return jnp.concatenate([s[j * rows:(j + 1) * rows] for j in range(nblk)], axis=1)


def _interleave(gens):
    gens = list(gens)
    while gens:
        for g in list(gens):
            try:
                next(g)
            except StopIteration:
                gens.remove(g)


def _token_kernel(*refs, n_mix):
    it = iter(refs)
    x = next(it)[...]
    if n_mix:
        o_refs = [next(it) for _ in range(n_mix)]
        w_ref, g_ref, b_ref = next(it), next(it), next(it)
        acc = DEEPNORM_ALPHA * x
        off = 0
        for o_ref in o_refs:
            n = o_ref.shape[1]
            acc = acc + jnp.dot(o_ref[...].astype(BF16), w_ref[off:off + n, :], preferred_element_type=F32)
            off += n
        x = _layer_norm(acc, g_ref[...], b_ref[...])
    wg_ref, wu_ref, wd_ref, g_ref, b_ref, out_ref = (next(it) for _ in range(6))
    xb = x.astype(BF16)
    gate = jnp.dot(xb, wg_ref[...], preferred_element_type=F32)
    up = jnp.dot(xb, wu_ref[...], preferred_element_type=F32)
    act = (_silu(gate) * up).astype(BF16)
    f = jnp.dot(act, wd_ref[...], preferred_element_type=F32)
    out_ref[...] = _layer_norm(DEEPNORM_ALPHA * x + 0.5 * f, g_ref[...], b_ref[...])


def _resident(block_shape, index_map):
    return pl.BlockSpec(block_shape, index_map, pipeline_mode=pl.Buffered(1))


def token_block(h, p, l, j, mix=None):
    t = h.shape[0]
    row = lambda n: pl.BlockSpec((ROW_TILE, n), lambda r: (r, 0))
    ln = lambda k: pl.BlockSpec((None, 1, D_MODEL), lambda r: (3 * l + k, 0, 0))
    args, in_specs = [h], [row(D_MODEL)]
    if mix:
        outs, w_out, i = mix
        args += list(outs) + [w_out, p["ln_g"], p["ln_b"]]
        in_specs += [row(o.shape[1]) for o in outs] + [_resident((None, D_MODEL, D_MODEL), lambda r: (i, 0, 0)), ln(1), ln(1)]
    args += [p["wg"], p["wu"], p["wd"], p["ln_g"], p["ln_b"]]
    in_specs += [_resident((None, None, D_MODEL, D_FF), lambda r: (l, j, 0, 0)),
                 _resident((None, None, D_MODEL, D_FF), lambda r: (l, j, 0, 0)),
                 _resident((None, None, D_FF, D_MODEL), lambda r: (l, j, 0, 0)), ln(2 * j), ln(2 * j)]
    return pl.pallas_call(
        functools.partial(_token_kernel, n_mix=len(mix[0]) if mix else 0),
        grid=(t // ROW_TILE,),
        in_specs=in_specs,
        out_specs=row(D_MODEL),
        out_shape=jax.ShapeDtypeStruct((t, D_MODEL), F32),
        compiler_params=_cparams(1),
        name="token_block",
    )(*args)


def _proj_kernel(h_ref, w_ref, *o_refs):
    xb = h_ref[...].astype(BF16)
    off = 0
    for o_ref in o_refs:
        n = o_ref.shape[1]
        o_ref[...] = jnp.dot(xb, w_ref[:, off:off + n], preferred_element_type=F32)
        off += n


def project(h, w, i, widths):
    t = h.shape[0]
    ncols = w.shape[2]
    assert sum(widths) == ncols
    return pl.pallas_call(
        _proj_kernel,
        grid=(t // ROW_TILE,),
        in_specs=[pl.BlockSpec((ROW_TILE, D_MODEL), lambda r: (r, 0)),
                  _resident((None, D_MODEL, ncols), lambda r: (i, 0, 0))],
        out_specs=[pl.BlockSpec((ROW_TILE, n), lambda r: (r, 0)) for n in widths],
        out_shape=[jax.ShapeDtypeStruct((t, n), F32) for n in widths],
        compiler_params=_cparams(1),
        name="project",
    )(h, w)


def _row_info(cfg, shape, dim):
    r = _iota(shape, dim)
    return r >> cfg.shift, r & (cfg.lb - 1)


def _seq_rows(cfg, s, base=0):
    return slice(base + s * cfg.lb, base + (s + 1) * cfg.lb)


def _pair_masks(cfg, n):
    ri, ci = _iota((n, n), 0), _iota((n, n), 1)
    same = (ri >> cfg.shift) == (ci >> cfg.shift)
    ti, tj = ri & (cfg.lb - 1), ci & (cfg.lb - 1)
    return same & (tj <= ti), same & (tj < ti), same


def _shifted(cfg, x, hist, t_c, s):
    rolled = pltpu.roll(x, s, 0)
    if cfg.nseq == 1:
        head = jnp.where(t_c[0:SUBLANES] >= s, rolled[0:SUBLANES], pltpu.roll(hist, s, 0))
        return jnp.concatenate([head, rolled[SUBLANES:]], axis=0)
    assert cfg.hist == cfg.rows
    return jnp.where(t_c >= s, rolled, pltpu.roll(hist, cfg.rows - SUBLANES + s, 0))


def _new_history(cfg, x):
    rolled = x if cfg.lb == cfg.valid else pltpu.roll(x, cfg.lb - cfg.valid, 0)
    return jnp.concatenate([rolled[(j + 1) * cfg.lb - SUBLANES:(j + 1) * cfg.lb] for j in range(cfg.nseq)], axis=0)


def _seq_call(kernel, cfg, name, rows_in, hist_in, state_in, params, rows_out, hist_out, state_out, scratch, i):
    t = rows_in[0].shape[0]
    nbt = t // (cfg.rows * cfg.nc)
    assert nbt % cfg.units == 0
    grid = (nbt // cfg.units, cfg.nc)

    def zeros(n):
        return (0,) * n

    row_spec = lambda n: pl.BlockSpec((cfg.units, None, cfg.rows, n), lambda b, c: (b, c, 0, 0))
    args = [a.reshape(nbt, cfg.nc, cfg.rows, a.shape[1]) for a in rows_in]
    in_specs = [row_spec(a.shape[1]) for a in rows_in]
    if cfg.has_state:
        for a in hist_in:
            args.append(a)
            in_specs.append(pl.BlockSpec((None, cfg.units * cfg.hist, a.shape[2]), lambda b, c: (i, b, 0)))
        for a in state_in:
            nd = a.ndim - 2
            args.append(a)
            in_specs.append(pl.BlockSpec((None, cfg.units * cfg.nseq) + a.shape[2:],
                                         lambda b, c, nd=nd: (i, b) + zeros(nd)))
    for a, bs, im in params:
        args.append(a)
        in_specs.append(pl.BlockSpec(bs, im))
    out_specs = [row_spec(n) for n in rows_out]
    out_shape = [jax.ShapeDtypeStruct((nbt, cfg.nc, cfg.rows, n), F32) for n in rows_out]
    for n in hist_out:
        out_specs.append(pl.BlockSpec((cfg.units * cfg.hist, n), lambda b, c: (b, 0)))
        out_shape.append(jax.ShapeDtypeStruct((nbt * cfg.hist, n), F32))
    for shp in state_out:
        out_specs.append(pl.BlockSpec((cfg.units * cfg.nseq,) + shp, lambda b, c, nd=len(shp): (b,) + zeros(nd)))
        out_shape.append(jax.ShapeDtypeStruct((nbt * cfg.nseq,) + shp, F32))
    outs = pl.pallas_call(
        functools.partial(kernel, cfg=cfg),
        grid=grid, in_specs=in_specs, out_specs=out_specs, out_shape=out_shape,
        scratch_shapes=scratch, compiler_params=_cparams(2), name=name,
    )(*args)
    return [o.reshape(t, o.shape[-1]) for o in outs[:len(rows_out)]] + list(outs[len(rows_out):])


def _ret_kernel(*refs, cfg):
    if cfg.has_state:
        zb_ref, s0_ref, g_ref, b_ref, o_ref, sout_ref, s_scr = refs
    else:
        zb_ref, g_ref, b_ref, o_ref, sout_ref, s_scr = refs
    c = pl.program_id(1)
    rows = cfg.rows

    @pl.when(c == 0)
    def _():
        s_scr[...] = s0_ref[...] if cfg.has_state else jnp.zeros(s_scr.shape, F32)

    seq_c, t_c = _row_info(cfg, (rows, 1), 0)
    live = t_c < cfg.valid
    incl, _, _ = _pair_masks(cfg, rows)
    ti = (_iota((rows, rows), 0) & (cfg.lb - 1))
    tj = (_iota((rows, rows), 1) & (cfg.lb - 1))
    steps_i = jnp.minimum(ti + 1, cfg.valid).astype(F32)
    steps_j = jnp.minimum(tj + 1, cfg.valid).astype(F32)
    steps_c = jnp.minimum(t_c + 1, cfg.valid).astype(F32)

    lane = _iota((rows, LANES), 1)
    half = DK_B // 2
    inv_freq = 1.0 / jnp.power(ROPE_THETA, (lane & (half - 1)).astype(F32) / (half - 1.0))
    pos = (cfg.pos0 + c * cfg.lb + t_c).astype(F32)
    ang = pos * inv_freq
    cos = jnp.cos(ang)
    sin = jnp.sin(ang)
    sin = jnp.where(lane < half, -sin, sin)

    def rot(x):
        return x * cos + pltpu.roll(x, half, 1) * sin

    def head(u, h):
        sl = slice(h * DK_B, (h + 1) * DK_B)
        q = rot(zb_ref[u, :, sl])
        k = rot(zb_ref[u, :, H_B * DK_B + h * DK_B:H_B * DK_B + (h + 1) * DK_B]) * DK_B ** -0.5
        v = zb_ref[u, :, 2 * H_B * DK_B + h * DV_B:2 * H_B * DK_B + (h + 1) * DV_B]
        gate = zb_ref[u, :, 2 * H_B * DK_B + C_B + h * DV_B:2 * H_B * DK_B + C_B + (h + 1) * DV_B]
        k = jnp.where(live, k, 0.0)
        v = jnp.where(live, v, 0.0)
        lam = math.log(1.0 - 2.0 ** (-5.0 - h))
        dmat = jnp.where(incl, jnp.exp(lam * (steps_i - steps_j)), 0.0)
        sc = _bdot_nt(q, k) * dmat
        yield
        o = _bdot(sc, v)
        kz = k * jnp.exp(lam * (cfg.valid - steps_c))
        xi = jnp.exp(lam * steps_c)
        cross = []
        for s in range(cfg.nseq):
            rs = _seq_rows(cfg, s)
            st = s_scr[u * cfg.nseq + s, h]
            cross.append(_bdot(q[rs], st))
            s_scr[u * cfg.nseq + s, h] = math.exp(lam * cfg.valid) * st + _bdot_tn(kz[rs], v[rs])
        o = o + jnp.concatenate(cross, axis=0) * xi
        yield
        oc = o - jnp.mean(o, -1, keepdims=True)
        var = jnp.mean(oc * oc, -1, keepdims=True)
        o = oc * lax.rsqrt(var + RET_GN_EPS) * g_ref[:, sl] + b_ref[:, sl]
        o_ref[u, :, sl] = o * _silu(gate)

    _interleave(head(u, h) for u in range(cfg.units) for h in range(H_B))

    @pl.when(c == cfg.nc - 1)
    def _():
        sout_ref[...] = s_scr[...]


def retention(zb, s0, gn_g, gn_b, i, cfg):
    par = ((None, 1, C_B), lambda b, c: (i, 0, 0))
    return _seq_call(_ret_kernel, cfg, "retention", [zb], [], [s0], [(gn_g,) + par, (gn_b,) + par],
                     [C_B], [], [(H_B, DK_B, DV_B)],
                     [pltpu.VMEM((cfg.units * cfg.nseq, H_B, DK_B, DV_B), F32)], i)


def _gdn_kernel(*refs, cfg):
    if cfg.has_state:
        (qkv_ref, ba_ref, zg_ref, pv_ref, s0_ref, cw_ref, alog_ref, dtb_ref, ng_ref,
         o_ref, cvout_ref, sout_ref, s_scr, pv_scr) = refs
    else:
        (qkv_ref, ba_ref, zg_ref, cw_ref, alog_ref, dtb_ref, ng_ref,
         o_ref, cvout_ref, sout_ref, s_scr, pv_scr) = refs
    c = pl.program_id(1)
    rows, lb, valid, nhist = cfg.rows, cfg.lb, cfg.valid, cfg.hist

    @pl.when(c == 0)
    def _():
        s_scr[...] = s0_ref[...] if cfg.has_state else jnp.zeros(s_scr.shape, F32)
        for u in range(cfg.units):
            pv_scr[u, 0:nhist] = (pv_ref[u * nhist:(u + 1) * nhist] if cfg.has_state
                                  else jnp.zeros((nhist, N_QKV_C), F32))

    seq_c, t_c = _row_info(cfg, (rows, 1), 0)
    live = t_c < valid
    incl_r, _, same_r = _pair_masks(cfg, rows)
    n = GDN_GROUP * rows
    incl, strict, _ = _pair_masks(cfg, n)

    def prep(u):
        x = qkv_ref[u]
        conv = cw_ref[CONV_W - 1:CONV_W, :] * x
        hist = pv_scr[u, 0:nhist]
        for s in range(1, CONV_W):
            conv = conv + cw_ref[CONV_W - 1 - s:CONV_W - s, :] * _shifted(cfg, x, hist, t_c, s)
        pv_scr[u, 0:nhist] = _new_history(cfg, x)
        qkv = _silu(conv)
        ba = ba_ref[u]
        beta_all = jnp.where(live, _sigmoid(ba[:, 0:LANES]), 0.0)
        g_all = jnp.where(live, -jnp.exp(alog_ref[...]) * _softplus(ba[:, LANES:2 * LANES] + dtb_ref[...]), 0.0)
        return qkv, beta_all, _mask_dot(incl_r, g_all), _mask_dot(same_r, g_all)

    def group(u, gi, qkv, beta_all, gc_all, gl_all):
        heads = range(gi * GDN_GROUP, (gi + 1) * GDN_GROUP)

        def stack(arr, base):
            return jnp.concatenate([arr[:, base + DK_C * h:base + DK_C * (h + 1)] for h in heads], axis=0)

        def stack_col(arr):
            return jnp.concatenate([arr[:, h:h + 1] for h in heads], axis=0)

        def unit(a):
            return a * lax.rsqrt(jnp.maximum(jnp.sum(a * a, -1, keepdims=True), 1e-12))

        qs = unit(stack(qkv, 0)) * DK_C ** -0.5
        ks = unit(stack(qkv, H_C * DK_C))
        vs = stack(qkv, 2 * H_C * DK_C)
        bs, gcs, gls = stack_col(beta_all), stack_col(gc_all), stack_col(gl_all)
        diff = gcs - gcs.T
        dmat = jnp.where(incl, jnp.exp(jnp.where(incl, diff, 0.0)), 0.0)
        kk = _bdot_nt(ks, ks)
        qk = _bdot_nt(qs, ks) * dmat
        yield
        a = jnp.where(strict, bs * kk * dmat, 0.0)
        tinv = yield from _unit_lower_inverse(-a, lb)
        egc = jnp.exp(gcs)
        uw = _bdot(tinv, jnp.concatenate([vs * bs, ks * (bs * egc)], axis=1))
        u_, w = uw[:, 0:DV_C], uw[:, DV_C:DV_C + DK_C]
        yield
        qg = qs * egc
        kd = ks * jnp.exp(gls - gcs)
        egl = jnp.exp(gls)
        v_new, o_state = [], []
        for hh, h in enumerate(heads):
            rs = slice(hh * rows, (hh + 1) * rows)
            ws, os_ = [], []
            for s in range(cfg.nseq):
                sr = _seq_rows(cfg, s, hh * rows)
                res = _bdot(jnp.concatenate([w[sr], qg[sr]], axis=0), s_scr[u * cfg.nseq + s, h])
                ws.append(res[0:lb])
                os_.append(res[lb:2 * lb])
            v_new.append(u_[rs] - jnp.concatenate(ws, axis=0))
            o_state.append(jnp.concatenate(os_, axis=0))
        yield
        v_new_s = jnp.concatenate(v_new, axis=0)
        o_s = jnp.concatenate(o_state, axis=0) + _bdot(qk, v_new_s)
        for hh, h in enumerate(heads):
            for s in range(cfg.nseq):
                sr = _seq_rows(cfg, s, hh * rows)
                r0 = hh * rows + s * lb
                s_scr[u * cfg.nseq + s, h] = (egl[r0:r0 + 1, :] * s_scr[u * cfg.nseq + s, h]
                                              + _bdot_tn(kd[sr], v_new_s[sr]))
        yield
        for hh, h in enumerate(heads):
            o = o_s[hh * rows:(hh + 1) * rows]
            o = o * lax.rsqrt(jnp.mean(o * o, -1, keepdims=True) + 1e-6) * ng_ref[...]
            o_ref[u, :, h * DV_C:(h + 1) * DV_C] = o * _silu(zg_ref[u, :, h * DV_C:(h + 1) * DV_C])

    preps = [prep(u) for u in range(cfg.units)]
    _interleave(group(u, gi, *preps[u]) for u in range(cfg.units) for gi in range(H_C // GDN_GROUP))

    @pl.when(c == cfg.nc - 1)
    def _():
        for u in range(cfg.units):
            cvout_ref[u * nhist:(u + 1) * nhist] = pv_scr[u, 0:nhist]
        sout_ref[...] = s_scr[...]


def gdn(qkv, ba, zg, conv_prev, s0, conv_w, a_log, dt_bias, norm_g, i, cfg):
    par = lambda n: ((None, 1, n), lambda b, c: (i, 0, 0))
    params = [(conv_w, (None, CONV_W, N_QKV_C), lambda b, c: (i, 0, 0)), (a_log,) + par(LANES),
              (dt_bias,) + par(LANES), (norm_g,) + par(DV_C)]
    return _seq_call(_gdn_kernel, cfg, "gdn", [qkv, ba, zg], [conv_prev], [s0], params,
                     [C_C], [N_QKV_C], [(H_C, DK_C, DV_C)],
                     [pltpu.VMEM((cfg.units * cfg.nseq, H_C, DK_C, DV_C), F32),
                      pltpu.VMEM((cfg.units, cfg.hist, N_QKV_C), F32)], i)


RWKV_VEC_ROWS = 8
RWKV_GL = RWKV_GROUP * N_A
RWKV_NG = H_A // RWKV_GROUP


def _rwkv_kernel(*refs, cfg):
    if cfg.has_state:
        (za_ref, pv_ref, s0_ref, mu_ref, vec_ref, lw_ref, ones_ref,
         o_ref, shout_ref, sout_ref, s_scr, pv_scr) = refs
    else:
        (za_ref, mu_ref, vec_ref, lw_ref, ones_ref, o_ref, shout_ref, sout_ref, s_scr, pv_scr) = refs
    c = pl.program_id(1)
    rows, lb, valid = cfg.rows, cfg.lb, cfg.valid
    n = RWKV_GROUP * rows
    head_shift = int(math.log2(N_A))
    row_shift = int(math.log2(rows))
    assert 1 << row_shift == rows

    bi, bj = _iota((RWKV_GL, RWKV_GL), 0), _iota((RWKV_GL, RWKV_GL), 1)
    blockdiag = (bi >> head_shift) == (bj >> head_shift)

    def slot(u, s, g):
        return (u * cfg.nseq + s) * RWKV_NG + g

    @pl.when(c == 0)
    def _():
        pv_scr[...] = pv_ref[...] if cfg.has_state else jnp.zeros(pv_scr.shape, F32)
        for u in range(cfg.units):
            for s in range(cfg.nseq):
                for g in range(RWKV_NG):
                    if cfg.has_state:
                        zero = jnp.zeros((N_A, N_A), F32)
                        s_scr[slot(u, s, g)] = jnp.concatenate(
                            [jnp.concatenate([s0_ref[u * cfg.nseq + s, g * RWKV_GROUP + hh] if hh == hc else zero
                                              for hc in range(RWKV_GROUP)], axis=1) for hh in range(RWKV_GROUP)], axis=0)
                    else:
                        s_scr[slot(u, s, g)] = jnp.zeros((RWKV_GL, RWKV_GL), F32)

    seq_c, t_c = _row_info(cfg, (rows, 1), 0)
    live = t_c < valid
    incl_r, _, same_r = _pair_masks(cfg, rows)
    tj = _iota((rows, rows), 1) & (lb - 1)
    sum_masks = jnp.concatenate([incl_r.astype(BF16), (same_r & (tj < lb // 2)).astype(BF16), same_r.astype(BF16)],
                                axis=0)
    hi, hl = _iota((n, RWKV_GL), 0), _iota((n, RWKV_GL), 1)
    headmask = (hi >> row_shift) == (hl >> head_shift)
    incl, strict, _ = _pair_masks(cfg, n)
    spread = lambda x: jnp.where(headmask, _tile_rows(x, RWKV_GROUP), 0.0)
    w0, a0, k_k, k_a, r_k, gn_g, gn_b = (vec_ref[j:j + 1, :] for j in range(7))
    ones_pair = ones_ref[...]

    def prep(u):
        za = za_ref[u]
        za_prev = _shifted(cfg, za, pv_scr[u * cfg.hist:(u + 1) * cfg.hist], t_c, 1)
        pv_scr[u * cfg.hist:(u + 1) * cfg.hist] = _new_history(cfg, za)
        zs = za + (za_prev - za) * mu_ref[...]
        r = zs[:, 0:C_A]
        k = zs[:, C_A:2 * C_A]
        v = zs[:, 2 * C_A:3 * C_A]
        lo = zs[:, 3 * C_A:A_SHIFT_COLS]
        ll = _iota((rows, LORA_COLS), 1)
        lo = jnp.where(ll < W_LORA, jnp.tanh(lo), jnp.where(ll < W_LORA + A_LORA, lo, _sigmoid(lo)))
        w_log = -_softplus(-(w0 + _bdot(lo, lw_ref[0]))) - 0.5
        logw = -jnp.exp(w_log)
        a = _sigmoid(a0 + _bdot(lo, lw_ref[1]))
        gate = _bdot(lo, lw_ref[2])
        kk = k * k_k
        k = k * (1.0 + (a - 1.0) * k_a)
        sums = _head_sums(jnp.concatenate([kk * kk, r * k * r_k], axis=0), ones_pair)
        kk = kk * lax.rsqrt(jnp.maximum(sums[0:rows], 1e-12))
        bonus = sums[rows:2 * rows] * v
        logw = jnp.where(live, logw, 0.0)
        kk = jnp.where(live, kk, 0.0)
        k = jnp.where(live, k, 0.0)
        v = jnp.where(live, v, 0.0)
        ka = kk * a
        sums = _mask_dot(sum_masks, logw)
        cl, cm, ct = sums[0:rows], sums[rows:2 * rows], sums[2 * rows:3 * rows]
        clp = cl - logw
        e_mid = jnp.exp(cm - cl)
        e_end = jnp.exp(ct - cl)
        vecs = dict(
            b0=kk * jnp.exp(clp), r0=r * jnp.exp(cl),
            bc=kk * jnp.exp(clp - cm), rc=r * jnp.exp(cl - cm),
            ac=-ka * e_mid, kc=k * e_mid,
            ah=-ka * e_end, kh=k * e_end, wc=jnp.exp(ct), v=v)
        return vecs, bonus, gate

    def group(u, g, q, outs):
        sl = slice(g * RWKV_GL, (g + 1) * RWKV_GL)
        gram = _bdot_nt(jnp.concatenate([spread(q["bc"][:, sl]), spread(q["rc"][:, sl])], axis=0),
                        jnp.concatenate([_tile_rows(q["ac"][:, sl], RWKV_GROUP),
                                         _tile_rows(q["kc"][:, sl], RWKV_GROUP)], axis=0))
        u0, o0 = [], []
        for s in range(cfg.nseq):
            sr = _seq_rows(cfg, s)
            res = _bdot_nt(jnp.concatenate([q["b0"][sr, sl], q["r0"][sr, sl]], axis=0), s_scr[slot(u, s, g)])
            u0.append(res[0:lb])
            o0.append(res[lb:2 * lb])
        u0 = jnp.concatenate(u0, axis=0)
        o0 = jnp.concatenate(o0, axis=0)
        yield
        x1 = jnp.where(strict, gram[0:n, 0:n], 0.0)
        x2 = jnp.where(strict, gram[0:n, n:2 * n], 0.0)
        x3 = jnp.where(incl, gram[n:2 * n, 0:n], 0.0)
        x4 = jnp.where(incl, gram[n:2 * n, n:2 * n], 0.0)
        vh = spread(q["v"][:, sl])
        xv = _bdot(jnp.concatenate([x2, x4], axis=0), vh)
        rhs = spread(u0) + xv[0:n]
        tinv = yield from _unit_lower_inverse(x1, lb)
        uh = _bdot(tinv, rhs)
        yield
        oh = _bdot(x3, uh) + xv[n:2 * n]
        outs[u][g] = o0 + _sum_row_blocks(oh, RWKV_GROUP)
        uc = _sum_row_blocks(uh, RWKV_GROUP)
        for s in range(cfg.nseq):
            sr = _seq_rows(cfg, s)
            upd = _bdot_tn(jnp.concatenate([uc[sr], q["v"][sr, sl]], axis=0),
                           jnp.concatenate([q["ah"][sr, sl], q["kh"][sr, sl]], axis=0))
            s_scr[slot(u, s, g)] = (s_scr[slot(u, s, g)] * q["wc"][s * lb:s * lb + 1, sl]
                                    + jnp.where(blockdiag, upd, 0.0))
        yield

    preps = [prep(u) for u in range(cfg.units)]
    outs = [[None] * RWKV_NG for _ in range(cfg.units)]
    _interleave(group(u, g, preps[u][0], outs) for u in range(cfg.units) for g in range(RWKV_NG))
    for u in range(cfg.units):
        _, bonus, gate = preps[u]
        o = jnp.concatenate(outs[u], axis=1)
        mean = _head_sums(o, ones_pair) * (1.0 / N_A)
        oc = o - mean
        var = _head_sums(oc * oc, ones_pair) * (1.0 / N_A)
        o = oc * lax.rsqrt(var + RWKV_GN_EPS) * gn_g + gn_b
        o_ref[u] = (o + bonus) * gate

    @pl.when(c == cfg.nc - 1)
    def _():
        shout_ref[...] = pv_scr[...]
        for u in range(cfg.units):
            for s in range(cfg.nseq):
                for g in range(RWKV_NG):
                    for hh in range(RWKV_GROUP):
                        sout_ref[u * cfg.nseq + s, g * RWKV_GROUP + hh] = s_scr[
                            slot(u, s, g), hh * N_A:(hh + 1) * N_A, hh * N_A:(hh + 1) * N_A]


def rwkv(za, shift_prev, s0, mu, vecs, lora_w, ones_blk, i, cfg):
    params = [(mu, (None, 1, A_SHIFT_COLS), lambda b, c: (i, 0, 0)),
              (vecs, (None, RWKV_VEC_ROWS, C_A), lambda b, c: (i, 0, 0)),
              (lora_w, (None, 3, LORA_COLS, C_A), lambda b, c: (i, 0, 0, 0)),
              (ones_blk, (LANES, LANES), lambda b, c: (0, 0))]
    return _seq_call(_rwkv_kernel, cfg, "rwkv", [za], [shift_prev], [s0], params,
                     [C_A], [A_SHIFT_COLS], [(H_A, N_A, N_A)],
                     [pltpu.VMEM((cfg.units * cfg.nseq * RWKV_NG, RWKV_GL, RWKV_GL), F32),
                      pltpu.VMEM((cfg.units * cfg.hist, A_SHIFT_COLS), F32)], i)


def _trunk(x, cfg_rwkv, cfg_ret, cfg_gdn, st, p):
    has = st is not None
    h = x
    new = dict(shift=[], rwkv=[], ret=[], conv=[], gdn=[])
    for l in range(DEPTH):
        i = l // 2
        if l % 2 == 0:
            h = token_block(h, p, l, 0)
            za, zb = project(h, p["ab_in"], i, (A_SHIFT_COLS, B_COLS))
            o_a, sh, s_rw = rwkv(za, st["shift"] if has else None, st["rwkv"] if has else None,
                                 p["mu"], p["rwkv_vecs"], p["lora_w"], p["ones_blk"], i, cfg_rwkv)
            o_b, s_rt = retention(zb, st["ret"] if has else None, p["ret_g"], p["ret_b"], i, cfg_ret)
            mix = ([o_a, o_b], p["ab_out"], i)
            new["shift"].append(sh)
            new["rwkv"].append(s_rw)
            new["ret"].append(s_rt)
        else:
            h = token_block(h, p, l, 0)
            qkv, ba, zg = project(h, p["gdn_in"], i, (N_QKV_C, 2 * LANES, C_C))
            o, cv, s_gd = gdn(qkv, ba, zg, st["conv"] if has else None, st["gdn"] if has else None,
                              p["conv_w"], p["a_log"], p["dt_bias"], p["norm_g"], i, cfg_gdn)
            mix = ([o], p["gdn_out"], i)
            new["conv"].append(cv)
            new["gdn"].append(s_gd)
        h = token_block(h, p, l, 1, mix=mix)
    return h, {k: jnp.stack(v) for k, v in new.items()}


def _lane_pad(x, n):
    return jnp.pad(x, [(0, 0)] * (x.ndim - 1) + [(0, n - x.shape[-1])])


def _history_tiles(x):
    nl, b, r, n = x.shape
    return jnp.pad(x, ((0, 0), (0, 0), (SUBLANES - r, 0), (0, 0))).reshape(nl, b * SUBLANES, n)


def kernel(x_prompt, x_sample, state_rwkv_shift, state_rwkv, state_ret, state_gdn_conv, state_gdn, ln_g, ln_b, ffn_w_gate, ffn_w_up, ffn_w_down, ab_w_in, ab_w_out, rwkv_mu, rwkv_w0, rwkv_w2, rwkv_a0, rwkv_a2, rwkv_g2, rwkv_k_k, rwkv_k_a, rwkv_r_k, rwkv_gn_g, rwkv_gn_b, ret_gn_g, ret_gn_b, gdn_w_in, gdn_conv_w, gdn_A_log, gdn_dt_bias, gdn_norm_g, gdn_w_out):
    nb, seq = x_prompt.shape[0], x_prompt.shape[1]
    ndec, dec_seq = x_sample.shape[0], x_sample.shape[1]
    nl = N_AB_LAYERS

    lora_w = jnp.zeros((nl, 3, LORA_COLS, C_A), F32)
    lora_w = (lora_w.at[:, 0, 0:W_LORA].set(rwkv_w2).at[:, 1, W_LORA:W_LORA + A_LORA].set(rwkv_a2)
              .at[:, 2, W_LORA + A_LORA:].set(rwkv_g2))
    gdn_in = jnp.concatenate([gdn_w_in[:, :, :N_QKV_C],
                              _lane_pad(gdn_w_in[:, :, N_QKV_C:N_QKV_C + H_C], LANES),
                              _lane_pad(gdn_w_in[:, :, N_QKV_C + H_C:N_QKV_C + 2 * H_C], LANES),
                              gdn_w_in[:, :, N_QKV_C + 2 * H_C:]], axis=2)
    lane_head = np.arange(LANES) // N_A
    p = dict(
        wg=ffn_w_gate.astype(BF16), wu=ffn_w_up.astype(BF16), wd=ffn_w_down.astype(BF16),
        ln_g=ln_g.reshape(DEPTH * 3, 1, D_MODEL), ln_b=ln_b.reshape(DEPTH * 3, 1, D_MODEL),
        ab_in=ab_w_in.astype(BF16), ab_out=ab_w_out.astype(BF16),
        gdn_in=gdn_in.astype(BF16), gdn_out=gdn_w_out.astype(BF16),
        mu=rwkv_mu.reshape(nl, 1, A_SHIFT_COLS),
        rwkv_vecs=jnp.stack([rwkv_w0, rwkv_a0, rwkv_k_k, rwkv_k_a, rwkv_r_k.reshape(nl, C_A), rwkv_gn_g, rwkv_gn_b,
                             jnp.zeros_like(rwkv_w0)], axis=1),
        lora_w=lora_w.astype(BF16),
        ones_blk=jnp.asarray((lane_head[:, None] == lane_head[None, :]).astype(np.float32)),
        ret_g=ret_gn_g.reshape(nl, 1, C_B), ret_b=ret_gn_b.reshape(nl, 1, C_B),
        conv_w=gdn_conv_w, a_log=_lane_pad(gdn_A_log, LANES)[:, None, :], dt_bias=_lane_pad(gdn_dt_bias, LANES)[:, None, :],
        norm_g=gdn_norm_g[:, None, :],
    )

    hp, sp = _trunk(x_prompt.reshape(nb * seq, D_MODEL),
                    SeqCfg(1, 64, 64, seq // 64, 0, False, UNITS_DEEP),
                    SeqCfg(1, 128, 128, seq // 128, 0, False, UNITS),
                    SeqCfg(1, 64, 64, seq // 64, 0, False, UNITS_DEEP), None, p)
    cfg = SeqCfg(SEQ_PER_UNIT, SAMPLE_PAD, dec_seq, 1, PAST_LEN, True, UNITS)
    xs = jnp.pad(x_sample, ((0, 0), (0, SAMPLE_PAD - dec_seq), (0, 0))).reshape(ndec * SAMPLE_PAD, D_MODEL)
    st = dict(shift=_history_tiles(state_rwkv_shift[:, :, None, :]),
              rwkv=state_rwkv,
              ret=state_ret,
              conv=_history_tiles(state_gdn_conv),
              gdn=state_gdn)
    hs, ss = _trunk(xs, cfg, cfg, cfg, st, p)

    def unpack(s, n):
        return (s["shift"].reshape(nl, n, SUBLANES, A_SHIFT_COLS)[:, :, SUBLANES - 1],
                s["rwkv"],
                s["ret"],
                s["conv"].reshape(N_C_LAYERS, n, SUBLANES, N_QKV_C)[:, :, SUBLANES - (CONV_W - 1):],
                s["gdn"])

    y_prompt = hp.reshape(nb, seq, D_MODEL)
    y_sample = hs.reshape(ndec, SAMPLE_PAD, D_MODEL)[:, :dec_seq]
    return (y_prompt, y_sample) + unpack(sp, nb) + unpack(ss, ndec)
```

```python
import functools
import math

import numpy as np
import jax
import jax.numpy as jnp
from jax import lax
from jax.experimental import pallas as pl
from jax.experimental.pallas import tpu as pltpu

F32 = jnp.float32
BF16 = jnp.bfloat16

D_MODEL = 1024
DEPTH = 4
PAST_LEN = 16384
N_AB_LAYERS = (DEPTH + 1) // 2
N_C_LAYERS = DEPTH // 2
H_A, N_A = 8, 64
C_A = H_A * N_A
W_LORA, A_LORA, G_LORA = 64, 64, 128
LORA_COLS = W_LORA + A_LORA + G_LORA
A_SHIFT_COLS = 3 * C_A + LORA_COLS
RWKV_GN_EPS = 64e-5
H_B, DK_B, DV_B = 4, 128, 128
C_B = H_B * DV_B
ROPE_THETA = 10000.0
RET_GN_EPS = 1e-5
B_COLS = 2 * H_B * DK_B + 2 * C_B
AB_COLS = A_SHIFT_COLS + B_COLS
H_C, DK_C, DV_C = 8, 128, 128
C_C = H_C * DV_C
CONV_W = 4
N_QKV_C = 2 * H_C * DK_C + C_C
D_FF = 2816
LN_EPS = 1e-5
DEEPNORM_ALPHA = (2 * DEPTH) ** 0.25

V7X_VMEM_BYTES = 64 * 1024 * 1024
LANES = 128
SUBLANES = 8
VMEM_LIMIT = V7X_VMEM_BYTES * 7 // 8

SAMPLE_PAD = SUBLANES
SEQ_PER_UNIT = 8
UNITS = 2
UNITS_DEEP = 4
ROW_TILE = 512
RWKV_GROUP = 2
GDN_GROUP = 2


class SeqCfg:
    def __init__(self, nseq, lb, valid, nc, pos0, has_state, units):
        self.nseq, self.lb, self.valid, self.nc, self.pos0, self.has_state = nseq, lb, valid, nc, pos0, has_state
        self.units = units
        self.rows = nseq * lb
        self.shift = int(math.log2(lb))
        self.hist = nseq * SUBLANES
        assert 1 << self.shift == lb and lb >= SUBLANES and CONV_W - 1 <= valid <= lb


def _cparams(ngrid):
    return pltpu.CompilerParams(dimension_semantics=("arbitrary",) * ngrid, vmem_limit_bytes=VMEM_LIMIT)


def _bdot(a, b):
    return jnp.dot(a.astype(BF16), b.astype(BF16), preferred_element_type=F32)


def _bdot_nt(a, b):
    return lax.dot_general(a.astype(BF16), b.astype(BF16), (((1,), (1,)), ((), ())), preferred_element_type=F32)


def _bdot_tn(a, b):
    return lax.dot_general(a.astype(BF16), b.astype(BF16), (((0,), (0,)), ((), ())), preferred_element_type=F32)


def _split3(x):
    x1 = x.astype(BF16)
    r1 = x - x1.astype(F32)
    x2 = r1.astype(BF16)
    return x1, x2, (r1 - x2.astype(F32)).astype(BF16)


def _mask_dot(m, x):
    mb = m.astype(BF16)
    x1, x2, x3 = _split3(x)
    return (jnp.dot(mb, x1, preferred_element_type=F32) + jnp.dot(mb, x2, preferred_element_type=F32)
            + jnp.dot(mb, x3, preferred_element_type=F32))


def _sigmoid(x):
    return 1.0 / (1.0 + jnp.exp(-x))


def _silu(x):
    return x * _sigmoid(x)


def _softplus(x):
    return jnp.maximum(x, 0.0) + jnp.log(1.0 + jnp.exp(-jnp.abs(x)))


def _layer_norm(y, g, b):
    yc = y - jnp.mean(y, -1, keepdims=True)
    var = jnp.mean(yc * yc, -1, keepdims=True)
    return yc * lax.rsqrt(var + LN_EPS) * g + b


def _iota(shape, dim):
    return lax.broadcasted_iota(jnp.int32, shape, dim)


def _tile_rows(x, n):
    return jnp.concatenate([x] * n, axis=0)


def _sum_row_blocks(x, n):
    r = x.shape[0] // n
    out = x[0:r]
    for i in range(1, n):
        out = out + x[i * r:(i + 1) * r]
    return out


def _unit_lower_inverse(x, lb):
    n = x.shape[0]
    eye = (_iota((n, n), 0) == _iota((n, n), 1)).astype(F32)
    p = eye + x
    rounds = int(math.log2(lb)) - 1
    y = _bdot(x, x)
    yield
    for r in range(rounds):
        if r == rounds - 1:
            p = p + _bdot(p, y)
        else:
            z = _bdot(jnp.concatenate([p, y], axis=0), y)
            p = p + z[0:n]
            y = z[n:2 * n]
        yield
    return p


def _head_sums(x, ones_pair):
    rows = x.shape[0]
    nblk = x.shape[1] // LANES
    st = jnp.concatenate([x[:, j * LANES:(j + 1) * LANES] for j in range(nblk)], axis=0)
    s = _bdot(st, ones_pair)
    return jnp.concatenate([s[j * rows:(j + 1) * rows] for j in range(nblk)], axis=1)


def _interleave(gens):
    gens = list(gens)
    while gens:
        for g in list(gens):
            try:
                next(g)
            except StopIteration:
                gens.remove(g)


def _segment_steps(seg_tiles, body):
    r = pl.program_id(0)
    off = 0
    for k, n in enumerate(seg_tiles):
        if len(seg_tiles) == 1:
            body(k)
        else:
            pl.when((r >= off) & (r < off + n))(functools.partial(body, k))
        off += n


def _segment_rows(seg_tiles, k, width):
    off = sum(seg_tiles[:k])
    last = seg_tiles[k] - 1
    return pl.BlockSpec((ROW_TILE, width), lambda r: (jnp.clip(r - off, 0, last), 0))


def _token_kernel(*refs, n_mix, seg_tiles):
    nseg = len(seg_tiles)
    it = iter(refs)
    seg_in = [[next(it) for _ in range(1 + n_mix)] for _ in range(nseg)]
    if n_mix:
        w_ref, g1_ref, b1_ref = next(it), next(it), next(it)
    wg_ref, wu_ref, wd_ref, g_ref, b_ref = (next(it) for _ in range(5))
    out_refs = [next(it) for _ in range(nseg)]

    def body(k):
        x = seg_in[k][0][...]
        if n_mix:
            acc = DEEPNORM_ALPHA * x
            off = 0
            for o_ref in seg_in[k][1:]:
                n = o_ref.shape[1]
                acc = acc + jnp.dot(o_ref[...].astype(BF16), w_ref[off:off + n, :], preferred_element_type=F32)
                off += n
            x = _layer_norm(acc, g1_ref[...], b1_ref[...])
        xb = x.astype(BF16)
        gate = jnp.dot(xb, wg_ref[...], preferred_element_type=F32)
        up = jnp.dot(xb, wu_ref[...], preferred_element_type=F32)
        act = (_silu(gate) * up).astype(BF16)
        f = jnp.dot(act, wd_ref[...], preferred_element_type=F32)
        out_refs[k][...] = _layer_norm(DEEPNORM_ALPHA * x + 0.5 * f, g_ref[...], b_ref[...])

    _segment_steps(seg_tiles, body)


def _resident(block_shape, index_map):
    return pl.BlockSpec(block_shape, index_map, pipeline_mode=pl.Buffered(1))


def token_block(hs, p, l, j, mix=None):
    seg_tiles = tuple(h.shape[0] // ROW_TILE for h in hs)
    ln = lambda k: pl.BlockSpec((None, 1, D_MODEL), lambda r: (3 * l + k, 0, 0))
    args, in_specs = [], []
    for k, h in enumerate(hs):
        seg = [h] + (list(mix[0][k]) if mix else [])
        args += seg
        in_specs += [_segment_rows(seg_tiles, k, a.shape[1]) for a in seg]
    if mix:
        _, w_out, i = mix
        args += [w_out, p["ln_g"], p["ln_b"]]
        in_specs += [_resident((None, D_MODEL, D_MODEL), lambda r: (i, 0, 0)), ln(1), ln(1)]
    args += [p["wg"], p["wu"], p["wd"], p["ln_g"], p["ln_b"]]
    in_specs += [_resident((None, None, D_MODEL, D_FF), lambda r: (l, j, 0, 0)),
                 _resident((None, None, D_MODEL, D_FF), lambda r: (l, j, 0, 0)),
                 _resident((None, None, D_FF, D_MODEL), lambda r: (l, j, 0, 0)), ln(2 * j), ln(2 * j)]
    return pl.pallas_call(
        functools.partial(_token_kernel, n_mix=len(mix[0][0]) if mix else 0, seg_tiles=seg_tiles),
        grid=(sum(seg_tiles),),
        in_specs=in_specs,
        out_specs=[_segment_rows(seg_tiles, k, D_MODEL) for k in range(len(hs))],
        out_shape=[jax.ShapeDtypeStruct(h.shape, F32) for h in hs],
        compiler_params=_cparams(1),
        name="token_block",
    )(*args)


def _proj_kernel(*refs, seg_tiles, n_out):
    nseg = len(seg_tiles)
    h_refs, w_ref, o_refs = refs[:nseg], refs[nseg], refs[nseg + 1:]

    def body(k):
        xb = h_refs[k][...].astype(BF16)
        off = 0
        for o_ref in o_refs[k * n_out:(k + 1) * n_out]:
            n = o_ref.shape[1]
            o_ref[...] = jnp.dot(xb, w_ref[:, off:off + n], preferred_element_type=F32)
            off += n

    _segment_steps(seg_tiles, body)


def project(hs, w, i, widths):
    seg_tiles = tuple(h.shape[0] // ROW_TILE for h in hs)
    ncols = w.shape[2]
    assert sum(widths) == ncols
    outs = pl.pallas_call(
        functools.partial(_proj_kernel, seg_tiles=seg_tiles, n_out=len(widths)),
        grid=(sum(seg_tiles),),
        in_specs=[_segment_rows(seg_tiles, k, D_MODEL) for k in range(len(hs))]
        + [_resident((None, D_MODEL, ncols), lambda r: (i, 0, 0))],
        out_specs=[_segment_rows(seg_tiles, k, n) for k in range(len(hs)) for n in widths],
        out_shape=[jax.ShapeDtypeStruct((h.shape[0], n), F32) for h in hs for n in widths],
        compiler_params=_cparams(1),
        name="project",
    )(*hs, w)
    return [outs[k * len(widths):(k + 1) * len(widths)] for k in range(len(hs))]


def _row_info(cfg, shape, dim):
    r = _iota(shape, dim)
    return r >> cfg.shift, r & (cfg.lb - 1)


def _seq_rows(cfg, s, base=0):
    return slice(base + s * cfg.lb, base + (s + 1) * cfg.lb)


def _pair_masks(cfg, n):
    ri, ci = _iota((n, n), 0), _iota((n, n), 1)
    same = (ri >> cfg.shift) == (ci >> cfg.shift)
    ti, tj = ri & (cfg.lb - 1), ci & (cfg.lb - 1)
    return same & (tj <= ti), same & (tj < ti), same


def _shifted(cfg, x, hist, t_c, s):
    rolled = pltpu.roll(x, s, 0)
    if cfg.nseq == 1:
        head = jnp.where(t_c[0:SUBLANES] >= s, rolled[0:SUBLANES], pltpu.roll(hist, s, 0))
        return jnp.concatenate([head, rolled[SUBLANES:]], axis=0)
    assert cfg.hist == cfg.rows
    return jnp.where(t_c >= s, rolled, pltpu.roll(hist, cfg.rows - SUBLANES + s, 0))


def _new_history(cfg, x):
    rolled = x if cfg.lb == cfg.valid else pltpu.roll(x, cfg.lb - cfg.valid, 0)
    return jnp.concatenate([rolled[(j + 1) * cfg.lb - SUBLANES:(j + 1) * cfg.lb] for j in range(cfg.nseq)], axis=0)


def _seq_call(kernel, cfg, name, rows_in, hist_in, state_in, params, rows_out, hist_out, state_out, scratch, i):
    t = rows_in[0].shape[0]
    nbt = t // (cfg.rows * cfg.nc)
    assert nbt % cfg.units == 0
    grid = (nbt // cfg.units, cfg.nc)

    def zeros(n):
        return (0,) * n

    row_spec = lambda n: pl.BlockSpec((cfg.units, None, cfg.rows, n), lambda b, c: (b, c, 0, 0))
    args = [a.reshape(nbt, cfg.nc, cfg.rows, a.shape[1]) for a in rows_in]
    in_specs = [row_spec(a.shape[1]) for a in rows_in]
    if cfg.has_state:
        for a in hist_in:
            args.append(a)
            in_specs.append(pl.BlockSpec((None, cfg.units * cfg.hist, a.shape[2]), lambda b, c: (i, b, 0)))
        for a in state_in:
            nd = a.ndim - 2
            args.append(a)
            in_specs.append(pl.BlockSpec((None, cfg.units * cfg.nseq) + a.shape[2:],
                                         lambda b, c, nd=nd: (i, b) + zeros(nd)))
    for a, bs, im in params:
        args.append(a)
        in_specs.append(pl.BlockSpec(bs, im))
    out_specs = [row_spec(n) for n in rows_out]
    out_shape = [jax.ShapeDtypeStruct((nbt, cfg.nc, cfg.rows, n), F32) for n in rows_out]
    for n in hist_out:
        out_specs.append(pl.BlockSpec((cfg.units * cfg.hist, n), lambda b, c: (b, 0)))
        out_shape.append(jax.ShapeDtypeStruct((nbt * cfg.hist, n), F32))
    for shp in state_out:
        out_specs.append(pl.BlockSpec((cfg.units * cfg.nseq,) + shp, lambda b, c, nd=len(shp): (b,) + zeros(nd)))
        out_shape.append(jax.ShapeDtypeStruct((nbt * cfg.nseq,) + shp, F32))
    outs = pl.pallas_call(
        functools.partial(kernel, cfg=cfg),
        grid=grid, in_specs=in_specs, out_specs=out_specs, out_shape=out_shape,
        scratch_shapes=scratch, compiler_params=_cparams(2), name=name,
    )(*args)
    return [o.reshape(t, o.shape[-1]) for o in outs[:len(rows_out)]] + list(outs[len(rows_out):])


def _ret_kernel(*refs, cfg):
    if cfg.has_state:
        zb_ref, s0_ref, g_ref, b_ref, o_ref, sout_ref, s_scr = refs
    else:
        zb_ref, g_ref, b_ref, o_ref, sout_ref, s_scr = refs
    c = pl.program_id(1)
    rows = cfg.rows

    @pl.when(c == 0)
    def _():
        s_scr[...] = s0_ref[...] if cfg.has_state else jnp.zeros(s_scr.shape, F32)

    seq_c, t_c = _row_info(cfg, (rows, 1), 0)
    live = t_c < cfg.valid
    incl, _, _ = _pair_masks(cfg, rows)
    ti = (_iota((rows, rows), 0) & (cfg.lb - 1))
    tj = (_iota((rows, rows), 1) & (cfg.lb - 1))
    steps_i = jnp.minimum(ti + 1, cfg.valid).astype(F32)
    steps_j = jnp.minimum(tj + 1, cfg.valid).astype(F32)
    steps_c = jnp.minimum(t_c + 1, cfg.valid).astype(F32)

    lane = _iota((rows, LANES), 1)
    half = DK_B // 2
    inv_freq = 1.0 / jnp.power(ROPE_THETA, (lane & (half - 1)).astype(F32) / (half - 1.0))
    pos = (cfg.pos0 + c * cfg.lb + t_c).astype(F32)
    ang = pos * inv_freq
    cos = jnp.cos(ang)
    sin = jnp.sin(ang)
    sin = jnp.where(lane < half, -sin, sin)

    def rot(x):
        return x * cos + pltpu.roll(x, half, 1) * sin

    def head(u, h):
        sl = slice(h * DK_B, (h + 1) * DK_B)
        q = rot(zb_ref[u, :, sl])
        k = rot(zb_ref[u, :, H_B * DK_B + h * DK_B:H_B * DK_B + (h + 1) * DK_B]) * DK_B ** -0.5
        v = zb_ref[u, :, 2 * H_B * DK_B + h * DV_B:2 * H_B * DK_B + (h + 1) * DV_B]
        gate = zb_ref[u, :, 2 * H_B * DK_B + C_B + h * DV_B:2 * H_B * DK_B + C_B + (h + 1) * DV_B]
        k = jnp.where(live, k, 0.0)
        v = jnp.where(live, v, 0.0)
        lam = math.log(1.0 - 2.0 ** (-5.0 - h))
        dmat = jnp.where(incl, jnp.exp(lam * (steps_i - steps_j)), 0.0)
        sc = _bdot_nt(q, k) * dmat
        yield
        o = _bdot(sc, v)
        kz = k * jnp.exp(lam * (cfg.valid - steps_c))
        xi = jnp.exp(lam * steps_c)
        cross = []
        for s in range(cfg.nseq):
            rs = _seq_rows(cfg, s)
            st = s_scr[u * cfg.nseq + s, h]
            cross.append(_bdot(q[rs], st))
            s_scr[u * cfg.nseq + s, h] = math.exp(lam * cfg.valid) * st + _bdot_tn(kz[rs], v[rs])
        o = o + jnp.concatenate(cross, axis=0) * xi
        yield
        oc = o - jnp.mean(o, -1, keepdims=True)
        var = jnp.mean(oc * oc, -1, keepdims=True)
        o = oc * lax.rsqrt(var + RET_GN_EPS) * g_ref[:, sl] + b_ref[:, sl]
        o_ref[u, :, sl] = o * _silu(gate)

    _interleave(head(u, h) for u in range(cfg.units) for h in range(H_B))

    @pl.when(c == cfg.nc - 1)
    def _():
        sout_ref[...] = s_scr[...]


def retention(zb, s0, gn_g, gn_b, i, cfg):
    par = ((None, 1, C_B), lambda b, c: (i, 0, 0))
    return _seq_call(_ret_kernel, cfg, "retention", [zb], [], [s0], [(gn_g,) + par, (gn_b,) + par],
                     [C_B], [], [(H_B, DK_B, DV_B)],
                     [pltpu.VMEM((cfg.units * cfg.nseq, H_B, DK_B, DV_B), F32)], i)


def _gdn_kernel(*refs, cfg):
    if cfg.has_state:
        (qkv_ref, ba_ref, zg_ref, pv_ref, s0_ref, cw_ref, alog_ref, dtb_ref, ng_ref,
         o_ref, cvout_ref, sout_ref, s_scr, pv_scr) = refs
    else:
        (qkv_ref, ba_ref, zg_ref, cw_ref, alog_ref, dtb_ref, ng_ref,
         o_ref, cvout_ref, sout_ref, s_scr, pv_scr) = refs
    c = pl.program_id(1)
    rows, lb, valid, nhist = cfg.rows, cfg.lb, cfg.valid, cfg.hist

    @pl.when(c == 0)
    def _():
        s_scr[...] = s0_ref[...] if cfg.has_state else jnp.zeros(s_scr.shape, F32)
        for u in range(cfg.units):
            pv_scr[u, 0:nhist] = (pv_ref[u * nhist:(u + 1) * nhist] if cfg.has_state
                                  else jnp.zeros((nhist, N_QKV_C), F32))

    seq_c, t_c = _row_info(cfg, (rows, 1), 0)
    live = t_c < valid
    incl_r, _, same_r = _pair_masks(cfg, rows)
    n = GDN_GROUP * rows
    incl, strict, _ = _pair_masks(cfg, n)

    def prep(u):
        x = qkv_ref[u]
        conv = cw_ref[CONV_W - 1:CONV_W, :] * x
        hist = pv_scr[u, 0:nhist]
        for s in range(1, CONV_W):
            conv = conv + cw_ref[CONV_W - 1 - s:CONV_W - s, :] * _shifted(cfg, x, hist, t_c, s)
        pv_scr[u, 0:nhist] = _new_history(cfg, x)
        qkv = _silu(conv)
        ba = ba_ref[u]
        beta_all = jnp.where(live, _sigmoid(ba[:, 0:LANES]), 0.0)
        g_all = jnp.where(live, -jnp.exp(alog_ref[...]) * _softplus(ba[:, LANES:2 * LANES] + dtb_ref[...]), 0.0)
        return qkv, beta_all, _mask_dot(incl_r, g_all), _mask_dot(same_r, g_all)

    def group(u, gi, qkv, beta_all, gc_all, gl_all):
        heads = range(gi * GDN_GROUP, (gi + 1) * GDN_GROUP)

        def stack(arr, base):
            return jnp.concatenate([arr[:, base + DK_C * h:base + DK_C * (h + 1)] for h in heads], axis=0)

        def stack_col(arr):
            return jnp.concatenate([arr[:, h:h + 1] for h in heads], axis=0)

        def unit(a):
            return a * lax.rsqrt(jnp.maximum(jnp.sum(a * a, -1, keepdims=True), 1e-12))

        qs = unit(stack(qkv, 0)) * DK_C ** -0.5
        ks = unit(stack(qkv, H_C * DK_C))
        vs = stack(qkv, 2 * H_C * DK_C)
        bs, gcs, gls = stack_col(beta_all), stack_col(gc_all), stack_col(gl_all)
        diff = gcs - gcs.T
        dmat = jnp.where(incl, jnp.exp(jnp.where(incl, diff, 0.0)), 0.0)
        kk = _bdot_nt(ks, ks)
        qk = _bdot_nt(qs, ks) * dmat
        yield
        a = jnp.where(strict, bs * kk * dmat, 0.0)
        tinv = yield from _unit_lower_inverse(-a, lb)
        egc = jnp.exp(gcs)
        uw = _bdot(tinv, jnp.concatenate([vs * bs, ks * (bs * egc)], axis=1))
        u_, w = uw[:, 0:DV_C], uw[:, DV_C:DV_C + DK_C]
        yield
        qg = qs * egc
        kd = ks * jnp.exp(gls - gcs)
        egl = jnp.exp(gls)
        v_new, o_state = [], []
        for hh, h in enumerate(heads):
            rs = slice(hh * rows, (hh + 1) * rows)
            ws, os_ = [], []
            for s in range(cfg.nseq):
                sr = _seq_rows(cfg, s, hh * rows)
                res = _bdot(jnp.concatenate([w[sr], qg[sr]], axis=0), s_scr[u * cfg.nseq + s, h])
                ws.append(res[0:lb])
                os_.append(res[lb:2 * lb])
            v_new.append(u_[rs] - jnp.concatenate(ws, axis=0))
            o_state.append(jnp.concatenate(os_, axis=0))
        yield
        v_new_s = jnp.concatenate(v_new, axis=0)
        o_s = jnp.concatenate(o_state, axis=0) + _bdot(qk, v_new_s)
        for hh, h in enumerate(heads):
            for s in range(cfg.nseq):
                sr = _seq_rows(cfg, s, hh * rows)
                r0 = hh * rows + s * lb
                s_scr[u * cfg.nseq + s, h] = (egl[r0:r0 + 1, :] * s_scr[u * cfg.nseq + s, h]
                                              + _bdot_tn(kd[sr], v_new_s[sr]))
        yield
        for hh, h in enumerate(heads):
            o = o_s[hh * rows:(hh + 1) * rows]
            o = o * lax.rsqrt(jnp.mean(o * o, -1, keepdims=True) + 1e-6) * ng_ref[...]
            o_ref[u, :, h * DV_C:(h + 1) * DV_C] = o * _silu(zg_ref[u, :, h * DV_C:(h + 1) * DV_C])

    preps = [prep(u) for u in range(cfg.units)]
    _interleave(group(u, gi, *preps[u]) for u in range(cfg.units) for gi in range(H_C // GDN_GROUP))

    @pl.when(c == cfg.nc - 1)
    def _():
        for u in range(cfg.units):
            cvout_ref[u * nhist:(u + 1) * nhist] = pv_scr[u, 0:nhist]
        sout_ref[...] = s_scr[...]


def gdn(qkv, ba, zg, conv_prev, s0, conv_w, a_log, dt_bias, norm_g, i, cfg):
    par = lambda n: ((None, 1, n), lambda b, c: (i, 0, 0))
    params = [(conv_w, (None, CONV_W, N_QKV_C), lambda b, c: (i, 0, 0)), (a_log,) + par(LANES),
              (dt_bias,) + par(LANES), (norm_g,) + par(DV_C)]
    return _seq_call(_gdn_kernel, cfg, "gdn", [qkv, ba, zg], [conv_prev], [s0], params,
                     [C_C], [N_QKV_C], [(H_C, DK_C, DV_C)],
                     [pltpu.VMEM((cfg.units * cfg.nseq, H_C, DK_C, DV_C), F32),
                      pltpu.VMEM((cfg.units, cfg.hist, N_QKV_C), F32)], i)


RWKV_VEC_ROWS = 8
RWKV_GL = RWKV_GROUP * N_A
RWKV_NG = H_A // RWKV_GROUP


def _rwkv_kernel(*refs, cfg):
    if cfg.has_state:
        (za_ref, pv_ref, s0_ref, mu_ref, vec_ref, lw_ref, ones_ref,
         o_ref, shout_ref, sout_ref, s_scr, pv_scr) = refs
    else:
        (za_ref, mu_ref, vec_ref, lw_ref, ones_ref, o_ref, shout_ref, sout_ref, s_scr, pv_scr) = refs
    c = pl.program_id(1)
    rows, lb, valid = cfg.rows, cfg.lb, cfg.valid
    n = RWKV_GROUP * rows
    head_shift = int(math.log2(N_A))
    row_shift = int(math.log2(rows))
    assert 1 << row_shift == rows

    bi, bj = _iota((RWKV_GL, RWKV_GL), 0), _iota((RWKV_GL, RWKV_GL), 1)
    blockdiag = (bi >> head_shift) == (bj >> head_shift)

    def slot(u, s, g):
        return (u * cfg.nseq + s) * RWKV_NG + g

    @pl.when(c == 0)
    def _():
        pv_scr[...] = pv_ref[...] if cfg.has_state else jnp.zeros(pv_scr.shape, F32)
        for u in range(cfg.units):
            for s in range(cfg.nseq):
                for g in range(RWKV_NG):
                    if cfg.has_state:
                        sc = s0_ref[u * cfg.nseq + s, :, g * RWKV_GL:(g + 1) * RWKV_GL]
                        s_scr[slot(u, s, g)] = jnp.where(blockdiag, _tile_rows(sc, RWKV_GROUP), 0.0)
                    else:
                        s_scr[slot(u, s, g)] = jnp.zeros((RWKV_GL, RWKV_GL), F32)

    seq_c, t_c = _row_info(cfg, (rows, 1), 0)
    live = t_c < valid
    incl_r, _, same_r = _pair_masks(cfg, rows)
    tj = _iota((rows, rows), 1) & (lb - 1)
    sum_masks = jnp.concatenate([incl_r.astype(BF16), (same_r & (tj < lb // 2)).astype(BF16), same_r.astype(BF16)],
                                axis=0)
    hi, hl = _iota((n, RWKV_GL), 0), _iota((n, RWKV_GL), 1)
    headmask = (hi >> row_shift) == (hl >> head_shift)
    incl, strict, _ = _pair_masks(cfg, n)
    spread = lambda x: jnp.where(headmask, _tile_rows(x, RWKV_GROUP), 0.0)
    w0, a0, k_k, k_a, r_k, gn_g, gn_b = (vec_ref[j:j + 1, :] for j in range(7))
    ones_pair = ones_ref[...]

    def prep(u):
        za = za_ref[u]
        za_prev = _shifted(cfg, za, pv_scr[u * cfg.hist:(u + 1) * cfg.hist], t_c, 1)
        pv_scr[u * cfg.hist:(u + 1) * cfg.hist] = _new_history(cfg, za)
        zs = za + (za_prev - za) * mu_ref[...]
        r = zs[:, 0:C_A]
        k = zs[:, C_A:2 * C_A]
        v = zs[:, 2 * C_A:3 * C_A]
        lo = zs[:, 3 * C_A:A_SHIFT_COLS]
        ll = _iota((rows, LORA_COLS), 1)
        lo = jnp.where(ll < W_LORA, jnp.tanh(lo), jnp.where(ll < W_LORA + A_LORA, lo, _sigmoid(lo)))
        w_log = -_softplus(-(w0 + _bdot(lo, lw_ref[0]))) - 0.5
        logw = -jnp.exp(w_log)
        a = _sigmoid(a0 + _bdot(lo, lw_ref[1]))
        gate = _bdot(lo, lw_ref[2])
        kk = k * k_k
        k = k * (1.0 + (a - 1.0) * k_a)
        sums = _head_sums(jnp.concatenate([kk * kk, r * k * r_k], axis=0), ones_pair)
        kk = kk * lax.rsqrt(jnp.maximum(sums[0:rows], 1e-12))
        bonus = sums[rows:2 * rows] * v
        logw = jnp.where(live, logw, 0.0)
        kk = jnp.where(live, kk, 0.0)
        k = jnp.where(live, k, 0.0)
        v = jnp.where(live, v, 0.0)
        ka = kk * a
        sums = _mask_dot(sum_masks, logw)
        cl, cm, ct = sums[0:rows], sums[rows:2 * rows], sums[2 * rows:3 * rows]
        clp = cl - logw
        e_mid = jnp.exp(cm - cl)
        e_end = jnp.exp(ct - cl)
        vecs = dict(
            b0=kk * jnp.exp(clp), r0=r * jnp.exp(cl),
            bc=kk * jnp.exp(clp - cm), rc=r * jnp.exp(cl - cm),
            ac=-ka * e_mid, kc=k * e_mid,
            ah=-ka * e_end, kh=k * e_end, wc=jnp.exp(ct), v=v)
        return vecs, bonus, gate

    def group(u, g, q, outs):
        sl = slice(g * RWKV_GL, (g + 1) * RWKV_GL)
        gram = _bdot_nt(jnp.concatenate([spread(q["bc"][:, sl]), spread(q["rc"][:, sl])], axis=0),
                        jnp.concatenate([_tile_rows(q["ac"][:, sl], RWKV_GROUP),
                                         _tile_rows(q["kc"][:, sl], RWKV_GROUP)], axis=0))
        u0, o0 = [], []
        for s in range(cfg.nseq):
            sr = _seq_rows(cfg, s)
            res = _bdot_nt(jnp.concatenate([q["b0"][sr, sl], q["r0"][sr, sl]], axis=0), s_scr[slot(u, s, g)])
            u0.append(res[0:lb])
            o0.append(res[lb:2 * lb])
        u0 = jnp.concatenate(u0, axis=0)
        o0 = jnp.concatenate(o0, axis=0)
        yield
        x1 = jnp.where(strict, gram[0:n, 0:n], 0.0)
        x2 = jnp.where(strict, gram[0:n, n:2 * n], 0.0)
        x3 = jnp.where(incl, gram[n:2 * n, 0:n], 0.0)
        x4 = jnp.where(incl, gram[n:2 * n, n:2 * n], 0.0)
        vh = spread(q["v"][:, sl])
        xv = _bdot(jnp.concatenate([x2, x4], axis=0), vh)
        rhs = spread(u0) + xv[0:n]
        tinv = yield from _unit_lower_inverse(x1, lb)
        uh = _bdot(tinv, rhs)
        yield
        oh = _bdot(x3, uh) + xv[n:2 * n]
        outs[u][g] = o0 + _sum_row_blocks(oh, RWKV_GROUP)
        uc = _sum_row_blocks(uh, RWKV_GROUP)
        for s in range(cfg.nseq):
            sr = _seq_rows(cfg, s)
            upd = _bdot_tn(jnp.concatenate([uc[sr], q["v"][sr, sl]], axis=0),
                           jnp.concatenate([q["ah"][sr, sl], q["kh"][sr, sl]], axis=0))
            s_scr[slot(u, s, g)] = (s_scr[slot(u, s, g)] * q["wc"][s * lb:s * lb + 1, sl]
                                    + jnp.where(blockdiag, upd, 0.0))
        yield

    preps = [prep(u) for u in range(cfg.units)]
    outs = [[None] * RWKV_NG for _ in range(cfg.units)]
    _interleave(group(u, g, preps[u][0], outs) for u in range(cfg.units) for g in range(RWKV_NG))
    for u in range(cfg.units):
        _, bonus, gate = preps[u]
        o = jnp.concatenate(outs[u], axis=1)
        mean = _head_sums(o, ones_pair) * (1.0 / N_A)
        oc = o - mean
        var = _head_sums(oc * oc, ones_pair) * (1.0 / N_A)
        o = oc * lax.rsqrt(var + RWKV_GN_EPS) * gn_g + gn_b
        o_ref[u] = (o + bonus) * gate

    @pl.when(c == cfg.nc - 1)
    def _():
        shout_ref[...] = pv_scr[...]
        for u in range(cfg.units):
            for s in range(cfg.nseq):
                for g in range(RWKV_NG):
                    sout_ref[u * cfg.nseq + s, :, g * RWKV_GL:(g + 1) * RWKV_GL] = _sum_row_blocks(
                        s_scr[slot(u, s, g)], RWKV_GROUP)


def rwkv(za, shift_prev, s0, mu, vecs, lora_w, ones_blk, i, cfg):
    params = [(mu, (None, 1, A_SHIFT_COLS), lambda b, c: (i, 0, 0)),
              (vecs, (None, RWKV_VEC_ROWS, C_A), lambda b, c: (i, 0, 0)),
              (lora_w, (None, 3, LORA_COLS, C_A), lambda b, c: (i, 0, 0, 0)),
              (ones_blk, (LANES, LANES), lambda b, c: (0, 0))]
    return _seq_call(_rwkv_kernel, cfg, "rwkv", [za], [shift_prev], [s0], params,
                     [C_A], [A_SHIFT_COLS], [(N_A, C_A)],
                     [pltpu.VMEM((cfg.units * cfg.nseq * RWKV_NG, RWKV_GL, RWKV_GL), F32),
                      pltpu.VMEM((cfg.units * cfg.hist, A_SHIFT_COLS), F32)], i)


def _trunk(xs, cfgs, sts, p):
    hs = list(xs)
    nseg = len(hs)
    new = [dict(shift=[], rwkv=[], ret=[], conv=[], gdn=[]) for _ in range(nseg)]
    for l in range(DEPTH):
        i = l // 2
        hs = token_block(hs, p, l, 0)
        outs = []
        if l % 2 == 0:
            zs = project(hs, p["ab_in"], i, (A_SHIFT_COLS, B_COLS))
            for k, ((za, zb), (cfg_rwkv, cfg_ret, _), st) in enumerate(zip(zs, cfgs, sts)):
                has = st is not None
                o_a, sh, s_rw = rwkv(za, st["shift"] if has else None, st["rwkv"] if has else None,
                                     p["mu"], p["rwkv_vecs"], p["lora_w"], p["ones_blk"], i, cfg_rwkv)
                o_b, s_rt = retention(zb, st["ret"] if has else None, p["ret_g"], p["ret_b"], i, cfg_ret)
                outs.append([o_a, o_b])
                new[k]["shift"].append(sh)
                new[k]["rwkv"].append(s_rw)
                new[k]["ret"].append(s_rt)
            w_out = p["ab_out"]
        else:
            zs = project(hs, p["gdn_in"], i, (N_QKV_C, 2 * LANES, C_C))
            for k, ((qkv, ba, zg), (_, _, cfg_gdn), st) in enumerate(zip(zs, cfgs, sts)):
                has = st is not None
                o, cv, s_gd = gdn(qkv, ba, zg, st["conv"] if has else None, st["gdn"] if has else None,
                                  p["conv_w"], p["a_log"], p["dt_bias"], p["norm_g"], i, cfg_gdn)
                outs.append([o])
                new[k]["conv"].append(cv)
                new[k]["gdn"].append(s_gd)
            w_out = p["gdn_out"]
        hs = token_block(hs, p, l, 1, mix=(outs, w_out, i))
    return hs, [{name: jnp.stack(v) for name, v in d.items()} for d in new]


def _lane_pad(x, n):
    return jnp.pad(x, [(0, 0)] * (x.ndim - 1) + [(0, n - x.shape[-1])])


def _history_tiles(x):
    nl, b, r, n = x.shape
    return jnp.pad(x, ((0, 0), (0, 0), (SUBLANES - r, 0), (0, 0))).reshape(nl, b * SUBLANES, n)


def kernel(x_prompt, x_sample, state_rwkv_shift, state_rwkv, state_ret, state_gdn_conv, state_gdn, ln_g, ln_b, ffn_w_gate, ffn_w_up, ffn_w_down, ab_w_in, ab_w_out, rwkv_mu, rwkv_w0, rwkv_w2, rwkv_a0, rwkv_a2, rwkv_g2, rwkv_k_k, rwkv_k_a, rwkv_r_k, rwkv_gn_g, rwkv_gn_b, ret_gn_g, ret_gn_b, gdn_w_in, gdn_conv_w, gdn_A_log, gdn_dt_bias, gdn_norm_g, gdn_w_out):
    nb, seq = x_prompt.shape[0], x_prompt.shape[1]
    ndec, dec_seq = x_sample.shape[0], x_sample.shape[1]
    nl = N_AB_LAYERS

    lora_w = jnp.zeros((nl, 3, LORA_COLS, C_A), F32)
    lora_w = (lora_w.at[:, 0, 0:W_LORA].set(rwkv_w2).at[:, 1, W_LORA:W_LORA + A_LORA].set(rwkv_a2)
              .at[:, 2, W_LORA + A_LORA:].set(rwkv_g2))
    gdn_in = jnp.concatenate([gdn_w_in[:, :, :N_QKV_C],
                              _lane_pad(gdn_w_in[:, :, N_QKV_C:N_QKV_C + H_C], LANES),
                              _lane_pad(gdn_w_in[:, :, N_QKV_C + H_C:N_QKV_C + 2 * H_C], LANES),
                              gdn_w_in[:, :, N_QKV_C + 2 * H_C:]], axis=2)
    lane_head = np.arange(LANES) // N_A
    p = dict(
        wg=ffn_w_gate.astype(BF16), wu=ffn_w_up.astype(BF16), wd=ffn_w_down.astype(BF16),
        ln_g=ln_g.reshape(DEPTH * 3, 1, D_MODEL), ln_b=ln_b.reshape(DEPTH * 3, 1, D_MODEL),
        ab_in=ab_w_in.astype(BF16), ab_out=ab_w_out.astype(BF16),
        gdn_in=gdn_in.astype(BF16), gdn_out=gdn_w_out.astype(BF16),
        mu=rwkv_mu.reshape(nl, 1, A_SHIFT_COLS),
        rwkv_vecs=jnp.stack([rwkv_w0, rwkv_a0, rwkv_k_k, rwkv_k_a, rwkv_r_k.reshape(nl, C_A), rwkv_gn_g, rwkv_gn_b,
                             jnp.zeros_like(rwkv_w0)], axis=1),
        lora_w=lora_w.astype(BF16),
        ones_blk=jnp.asarray((lane_head[:, None] == lane_head[None, :]).astype(np.float32)),
        ret_g=ret_gn_g.reshape(nl, 1, C_B), ret_b=ret_gn_b.reshape(nl, 1, C_B),
        conv_w=gdn_conv_w, a_log=_lane_pad(gdn_A_log, LANES)[:, None, :], dt_bias=_lane_pad(gdn_dt_bias, LANES)[:, None, :],
        norm_g=gdn_norm_g[:, None, :],
    )

    cfg_p = (SeqCfg(1, 64, 64, seq // 64, 0, False, UNITS_DEEP),
             SeqCfg(1, 128, 128, seq // 128, 0, False, UNITS),
             SeqCfg(1, 64, 64, seq // 64, 0, False, UNITS_DEEP))
    cfg = SeqCfg(SEQ_PER_UNIT, SAMPLE_PAD, dec_seq, 1, PAST_LEN, True, UNITS)
    xs = jnp.pad(x_sample, ((0, 0), (0, SAMPLE_PAD - dec_seq), (0, 0))).reshape(ndec * SAMPLE_PAD, D_MODEL)
    st = dict(shift=_history_tiles(state_rwkv_shift[:, :, None, :]),
              rwkv=state_rwkv.transpose(0, 1, 3, 2, 4).reshape(nl, ndec, N_A, C_A),
              ret=state_ret,
              conv=_history_tiles(state_gdn_conv),
              gdn=state_gdn)
    (hp, hs), (sp, ss) = _trunk([x_prompt.reshape(nb * seq, D_MODEL), xs], [cfg_p, (cfg, cfg, cfg)], [None, st], p)

    def unpack(s, n):
        return (s["shift"].reshape(nl, n, SUBLANES, A_SHIFT_COLS)[:, :, SUBLANES - 1],
                s["rwkv"].reshape(nl, n, N_A, H_A, N_A).transpose(0, 1, 3, 2, 4),
                s["ret"],
                s["conv"].reshape(N_C_LAYERS, n, SUBLANES, N_QKV_C)[:, :, SUBLANES - (CONV_W - 1):],
                s["gdn"])

    y_prompt = hp.reshape(nb, seq, D_MODEL)
    y_sample = hs.reshape(ndec, SAMPLE_PAD, D_MODEL)[:, :dec_seq]
    return (y_prompt, y_sample) + unpack(sp, nb) + unpack(ss, ndec)
```

```python
import functools
import math

import numpy as np
import jax
import jax.numpy as jnp
from jax import lax
from jax.experimental import pallas as pl
from jax.experimental.pallas import tpu as pltpu

F32 = jnp.float32
BF16 = jnp.bfloat16

D_MODEL = 1024
DEPTH = 4
PAST_LEN = 16384
N_AB_LAYERS = (DEPTH + 1) // 2
N_C_LAYERS = DEPTH // 2
H_A, N_A = 8, 64
C_A = H_A * N_A
W_LORA, A_LORA, G_LORA = 64, 64, 128
LORA_COLS = W_LORA + A_LORA + G_LORA
A_SHIFT_COLS = 3 * C_A + LORA_COLS
RWKV_GN_EPS = 64e-5
H_B, DK_B, DV_B = 4, 128, 128
C_B = H_B * DV_B
ROPE_THETA = 10000.0
RET_GN_EPS = 1e-5
B_COLS = 2 * H_B * DK_B + 2 * C_B
AB_COLS = A_SHIFT_COLS + B_COLS
H_C, DK_C, DV_C = 8, 128, 128
C_C = H_C * DV_C
CONV_W = 4
N_QKV_C = 2 * H_C * DK_C + C_C
D_FF = 2816
LN_EPS = 1e-5
DEEPNORM_ALPHA = (2 * DEPTH) ** 0.25

V7X_VMEM_BYTES = 64 * 1024 * 1024
LANES = 128
SUBLANES = 8
VMEM_LIMIT = V7X_VMEM_BYTES * 7 // 8

SAMPLE_PAD = SUBLANES
SEQ_PER_UNIT = 8
UNITS = 2
UNITS_DEEP = 8
ROW_TILE = 512
RWKV_GROUP = 2
GDN_GROUP = 2


class SeqCfg:
    def __init__(self, nseq, lb, valid, nc, pos0, has_state, units):
        self.nseq, self.lb, self.valid, self.nc, self.pos0, self.has_state = nseq, lb, valid, nc, pos0, has_state
        self.units = units
        self.rows = nseq * lb
        self.shift = int(math.log2(lb))
        self.hist = nseq * SUBLANES
        assert 1 << self.shift == lb and lb >= SUBLANES and CONV_W - 1 <= valid <= lb


def _cparams(ngrid):
    return pltpu.CompilerParams(dimension_semantics=("arbitrary",) * ngrid, vmem_limit_bytes=VMEM_LIMIT)


def _bdot(a, b):
    return jnp.dot(a.astype(BF16), b.astype(BF16), preferred_element_type=F32)


def _bdot_nt(a, b):
    return lax.dot_general(a.astype(BF16), b.astype(BF16), (((1,), (1,)), ((), ())), preferred_element_type=F32)


def _bdot_tn(a, b):
    return lax.dot_general(a.astype(BF16), b.astype(BF16), (((0,), (0,)), ((), ())), preferred_element_type=F32)


def _split3(x):
    x1 = x.astype(BF16)
    r1 = x - x1.astype(F32)
    x2 = r1.astype(BF16)
    return x1, x2, (r1 - x2.astype(F32)).astype(BF16)


def _mask_dot(m, x):
    mb = m.astype(BF16)
    x1, x2, x3 = _split3(x)
    return (jnp.dot(mb, x1, preferred_element_type=F32) + jnp.dot(mb, x2, preferred_element_type=F32)
            + jnp.dot(mb, x3, preferred_element_type=F32))


def _sigmoid(x):
    return 1.0 / (1.0 + jnp.exp(-x))


def _silu(x):
    return x * _sigmoid(x)


def _softplus(x):
    return jnp.maximum(x, 0.0) + jnp.log(1.0 + jnp.exp(-jnp.abs(x)))


def _layer_norm(y, g, b):
    yc = y - jnp.mean(y, -1, keepdims=True)
    var = jnp.mean(yc * yc, -1, keepdims=True)
    return yc * lax.rsqrt(var + LN_EPS) * g + b


def _iota(shape, dim):
    return lax.broadcasted_iota(jnp.int32, shape, dim)


def _tile_rows(x, n):
    return jnp.concatenate([x] * n, axis=0)


def _sum_row_blocks(x, n):
    r = x.shape[0] // n
    out = x[0:r]
    for i in range(1, n):
        out = out + x[i * r:(i + 1) * r]
    return out


def _unit_lower_inverse(x, lb):
    n = x.shape[0]
    eye = (_iota((n, n), 0) == _iota((n, n), 1)).astype(F32)
    p = eye + x
    rounds = int(math.log2(lb)) - 1
    y = _bdot(x, x)
    yield
    for r in range(rounds):
        if r == rounds - 1:
            p = p + _bdot(p, y)
        else:
            z = _bdot(jnp.concatenate([p, y], axis=0), y)
            p = p + z[0:n]
            y = z[n:2 * n]
        yield
    return p


def _head_sums(x, ones_pair):
    rows = x.shape[0]
    nblk = x.shape[1] // LANES
    st = jnp.concatenate([x[:, j * LANES:(j + 1) * LANES] for j in range(nblk)], axis=0)
    s = _bdot(st, ones_pair)
    return jnp.concatenate([s[j * rows:(j + 1) * rows] for j in range(nblk)], axis=1)


def _interleave(gens):
    gens = list(gens)
    while gens:
        for g in list(gens):
            try:
                next(g)
            except StopIteration:
                gens.remove(g)


def _segment_steps(seg_tiles, body):
    r = pl.program_id(0)
    off = 0
    for k, n in enumerate(seg_tiles):
        if len(seg_tiles) == 1:
            body(k)
        else:
            pl.when((r >= off) & (r < off + n))(functools.partial(body, k))
        off += n


def _segment_rows(seg_tiles, k, width):
    off = sum(seg_tiles[:k])
    last = seg_tiles[k] - 1
    return pl.BlockSpec((ROW_TILE, width), lambda r: (jnp.clip(r - off, 0, last), 0))


def _token_kernel(*refs, n_mix, seg_tiles):
    nseg = len(seg_tiles)
    it = iter(refs)
    seg_in = [[next(it) for _ in range(1 + n_mix)] for _ in range(nseg)]
    if n_mix:
        w_ref, g1_ref, b1_ref = next(it), next(it), next(it)
    wg_ref, wu_ref, wd_ref, g_ref, b_ref = (next(it) for _ in range(5))
    out_refs = [next(it) for _ in range(nseg)]

    def body(k):
        x = seg_in[k][0][...]
        if n_mix:
            acc = DEEPNORM_ALPHA * x
            off = 0
            for o_ref in seg_in[k][1:]:
                n = o_ref.shape[1]
                acc = acc + jnp.dot(o_ref[...].astype(BF16), w_ref[off:off + n, :], preferred_element_type=F32)
                off += n
            x = _layer_norm(acc, g1_ref[...], b1_ref[...])
        xb = x.astype(BF16)
        gate = jnp.dot(xb, wg_ref[...], preferred_element_type=F32)
        up = jnp.dot(xb, wu_ref[...], preferred_element_type=F32)
        act = (_silu(gate) * up).astype(BF16)
        f = jnp.dot(act, wd_ref[...], preferred_element_type=F32)
        out_refs[k][...] = _layer_norm(DEEPNORM_ALPHA * x + 0.5 * f, g_ref[...], b_ref[...])

    _segment_steps(seg_tiles, body)


def _resident(block_shape, index_map):
    return pl.BlockSpec(block_shape, index_map, pipeline_mode=pl.Buffered(1))


def token_block(hs, p, l, j, mix=None):
    seg_tiles = tuple(h.shape[0] // ROW_TILE for h in hs)
    ln = lambda k: pl.BlockSpec((None, 1, D_MODEL), lambda r: (3 * l + k, 0, 0))
    args, in_specs = [], []
    for k, h in enumerate(hs):
        seg = [h] + (list(mix[0][k]) if mix else [])
        args += seg
        in_specs += [_segment_rows(seg_tiles, k, a.shape[1]) for a in seg]
    if mix:
        _, w_out, i = mix
        args += [w_out, p["ln_g"], p["ln_b"]]
        in_specs += [_resident((None, D_MODEL, D_MODEL), lambda r: (i, 0, 0)), ln(1), ln(1)]
    args += [p["wg"], p["wu"], p["wd"], p["ln_g"], p["ln_b"]]
    in_specs += [_resident((None, None, D_MODEL, D_FF), lambda r: (l, j, 0, 0)),
                 _resident((None, None, D_MODEL, D_FF), lambda r: (l, j, 0, 0)),
                 _resident((None, None, D_FF, D_MODEL), lambda r: (l, j, 0, 0)), ln(2 * j), ln(2 * j)]
    return pl.pallas_call(
        functools.partial(_token_kernel, n_mix=len(mix[0][0]) if mix else 0, seg_tiles=seg_tiles),
        grid=(sum(seg_tiles),),
        in_specs=in_specs,
        out_specs=[_segment_rows(seg_tiles, k, D_MODEL) for k in range(len(hs))],
        out_shape=[jax.ShapeDtypeStruct(h.shape, F32) for h in hs],
        compiler_params=_cparams(1),
        name="token_block",
    )(*args)


def _proj_kernel(*refs, seg_tiles, n_out):
    nseg = len(seg_tiles)
    h_refs, w_ref, o_refs = refs[:nseg], refs[nseg], refs[nseg + 1:]

    def body(k):
        xb = h_refs[k][...].astype(BF16)
        off = 0
        for o_ref in o_refs[k * n_out:(k + 1) * n_out]:
            n = o_ref.shape[1]
            o_ref[...] = jnp.dot(xb, w_ref[:, off:off + n], preferred_element_type=F32)
            off += n

    _segment_steps(seg_tiles, body)


def project(hs, w, i, widths):
    seg_tiles = tuple(h.shape[0] // ROW_TILE for h in hs)
    ncols = w.shape[2]
    assert sum(widths) == ncols
    outs = pl.pallas_call(
        functools.partial(_proj_kernel, seg_tiles=seg_tiles, n_out=len(widths)),
        grid=(sum(seg_tiles),),
        in_specs=[_segment_rows(seg_tiles, k, D_MODEL) for k in range(len(hs))]
        + [_resident((None, D_MODEL, ncols), lambda r: (i, 0, 0))],
        out_specs=[_segment_rows(seg_tiles, k, n) for k in range(len(hs)) for n in widths],
        out_shape=[jax.ShapeDtypeStruct((h.shape[0], n), F32) for h in hs for n in widths],
        compiler_params=_cparams(1),
        name="project",
    )(*hs, w)
    return [outs[k * len(widths):(k + 1) * len(widths)] for k in range(len(hs))]


def _row_info(cfg, shape, dim):
    r = _iota(shape, dim)
    return r >> cfg.shift, r & (cfg.lb - 1)


def _seq_rows(cfg, s, base=0):
    return slice(base + s * cfg.lb, base + (s + 1) * cfg.lb)


def _pair_masks(cfg, n):
    ri, ci = _iota((n, n), 0), _iota((n, n), 1)
    same = (ri >> cfg.shift) == (ci >> cfg.shift)
    ti, tj = ri & (cfg.lb - 1), ci & (cfg.lb - 1)
    return same & (tj <= ti), same & (tj < ti), same


def _shifted(cfg, x, hist, t_c, s):
    rolled = pltpu.roll(x, s, 0)
    if cfg.nseq == 1:
        head = jnp.where(t_c[0:SUBLANES] >= s, rolled[0:SUBLANES], pltpu.roll(hist, s, 0))
        return jnp.concatenate([head, rolled[SUBLANES:]], axis=0)
    assert cfg.hist == cfg.rows
    return jnp.where(t_c >= s, rolled, pltpu.roll(hist, cfg.rows - SUBLANES + s, 0))


def _new_history(cfg, x):
    rolled = x if cfg.lb == cfg.valid else pltpu.roll(x, cfg.lb - cfg.valid, 0)
    return jnp.concatenate([rolled[(j + 1) * cfg.lb - SUBLANES:(j + 1) * cfg.lb] for j in range(cfg.nseq)], axis=0)


def _seq_call(kernel, cfg, name, rows_in, hist_in, state_in, params, rows_out, hist_out, state_out, scratch, i):
    t = rows_in[0].shape[0]
    nbt = t // (cfg.rows * cfg.nc)
    assert nbt % cfg.units == 0
    grid = (nbt // cfg.units, cfg.nc)

    def zeros(n):
        return (0,) * n

    row_spec = lambda n: pl.BlockSpec((cfg.units, None, cfg.rows, n), lambda b, c: (b, c, 0, 0))
    args = [a.reshape(nbt, cfg.nc, cfg.rows, a.shape[1]) for a in rows_in]
    in_specs = [row_spec(a.shape[1]) for a in rows_in]
    if cfg.has_state:
        for a in hist_in:
            args.append(a)
            in_specs.append(pl.BlockSpec((None, cfg.units * cfg.hist, a.shape[2]), lambda b, c: (i, b, 0)))
        for a in state_in:
            nd = a.ndim - 2
            args.append(a)
            in_specs.append(pl.BlockSpec((None, cfg.units * cfg.nseq) + a.shape[2:],
                                         lambda b, c, nd=nd: (i, b) + zeros(nd)))
    for a, bs, im in params:
        args.append(a)
        in_specs.append(pl.BlockSpec(bs, im))
    out_specs = [row_spec(n) for n in rows_out]
    out_shape = [jax.ShapeDtypeStruct((nbt, cfg.nc, cfg.rows, n), F32) for n in rows_out]
    for n in hist_out:
        out_specs.append(pl.BlockSpec((cfg.units * cfg.hist, n), lambda b, c: (b, 0)))
        out_shape.append(jax.ShapeDtypeStruct((nbt * cfg.hist, n), F32))
    for shp in state_out:
        out_specs.append(pl.BlockSpec((cfg.units * cfg.nseq,) + shp, lambda b, c, nd=len(shp): (b,) + zeros(nd)))
        out_shape.append(jax.ShapeDtypeStruct((nbt * cfg.nseq,) + shp, F32))
    outs = pl.pallas_call(
        functools.partial(kernel, cfg=cfg),
        grid=grid, in_specs=in_specs, out_specs=out_specs, out_shape=out_shape,
        scratch_shapes=scratch, compiler_params=_cparams(2), name=name,
    )(*args)
    return [o.reshape(t, o.shape[-1]) for o in outs[:len(rows_out)]] + list(outs[len(rows_out):])


def _ret_kernel(*refs, cfg):
    if cfg.has_state:
        zb_ref, s0_ref, g_ref, b_ref, o_ref, sout_ref, s_scr = refs
    else:
        zb_ref, g_ref, b_ref, o_ref, sout_ref, s_scr = refs
    c = pl.program_id(1)
    rows = cfg.rows

    @pl.when(c == 0)
    def _():
        s_scr[...] = s0_ref[...] if cfg.has_state else jnp.zeros(s_scr.shape, F32)

    seq_c, t_c = _row_info(cfg, (rows, 1), 0)
    live = t_c < cfg.valid
    incl, _, _ = _pair_masks(cfg, rows)
    ti = (_iota((rows, rows), 0) & (cfg.lb - 1))
    tj = (_iota((rows, rows), 1) & (cfg.lb - 1))
    steps_i = jnp.minimum(ti + 1, cfg.valid).astype(F32)
    steps_j = jnp.minimum(tj + 1, cfg.valid).astype(F32)
    steps_c = jnp.minimum(t_c + 1, cfg.valid).astype(F32)

    lane = _iota((rows, LANES), 1)
    half = DK_B // 2
    inv_freq = 1.0 / jnp.power(ROPE_THETA, (lane & (half - 1)).astype(F32) / (half - 1.0))
    pos = (cfg.pos0 + c * cfg.lb + t_c).astype(F32)
    ang = pos * inv_freq
    cos = jnp.cos(ang)
    sin = jnp.sin(ang)
    sin = jnp.where(lane < half, -sin, sin)

    def rot(x):
        return x * cos + pltpu.roll(x, half, 1) * sin

    def head(u, h):
        sl = slice(h * DK_B, (h + 1) * DK_B)
        q = rot(zb_ref[u, :, sl])
        k = rot(zb_ref[u, :, H_B * DK_B + h * DK_B:H_B * DK_B + (h + 1) * DK_B]) * DK_B ** -0.5
        v = zb_ref[u, :, 2 * H_B * DK_B + h * DV_B:2 * H_B * DK_B + (h + 1) * DV_B]
        gate = zb_ref[u, :, 2 * H_B * DK_B + C_B + h * DV_B:2 * H_B * DK_B + C_B + (h + 1) * DV_B]
        k = jnp.where(live, k, 0.0)
        v = jnp.where(live, v, 0.0)
        lam = math.log(1.0 - 2.0 ** (-5.0 - h))
        dmat = jnp.where(incl, jnp.exp(lam * (steps_i - steps_j)), 0.0)
        sc = _bdot_nt(q, k) * dmat
        yield
        o = _bdot(sc, v)
        kz = k * jnp.exp(lam * (cfg.valid - steps_c))
        xi = jnp.exp(lam * steps_c)
        cross = []
        for s in range(cfg.nseq):
            rs = _seq_rows(cfg, s)
            st = s_scr[u * cfg.nseq + s, h]
            cross.append(_bdot(q[rs], st))
            s_scr[u * cfg.nseq + s, h] = math.exp(lam * cfg.valid) * st + _bdot_tn(kz[rs], v[rs])
        o = o + jnp.concatenate(cross, axis=0) * xi
        yield
        oc = o - jnp.mean(o, -1, keepdims=True)
        var = jnp.mean(oc * oc, -1, keepdims=True)
        o = oc * lax.rsqrt(var + RET_GN_EPS) * g_ref[:, sl] + b_ref[:, sl]
        o_ref[u, :, sl] = o * _silu(gate)

    _interleave(head(u, h) for u in range(cfg.units) for h in range(H_B))

    @pl.when(c == cfg.nc - 1)
    def _():
        sout_ref[...] = s_scr[...]


def retention(zb, s0, gn_g, gn_b, i, cfg):
    par = ((None, 1, C_B), lambda b, c: (i, 0, 0))
    return _seq_call(_ret_kernel, cfg, "retention", [zb], [], [s0], [(gn_g,) + par, (gn_b,) + par],
                     [C_B], [], [(H_B, DK_B, DV_B)],
                     [pltpu.VMEM((cfg.units * cfg.nseq, H_B, DK_B, DV_B), F32)], i)


def _gdn_kernel(*refs, cfg):
    if cfg.has_state:
        (qkv_ref, ba_ref, zg_ref, pv_ref, s0_ref, cw_ref, alog_ref, dtb_ref, ng_ref,
         o_ref, cvout_ref, sout_ref, s_scr, pv_scr) = refs
    else:
        (qkv_ref, ba_ref, zg_ref, cw_ref, alog_ref, dtb_ref, ng_ref,
         o_ref, cvout_ref, sout_ref, s_scr, pv_scr) = refs
    c = pl.program_id(1)
    rows, lb, valid, nhist = cfg.rows, cfg.lb, cfg.valid, cfg.hist

    @pl.when(c == 0)
    def _():
        s_scr[...] = s0_ref[...] if cfg.has_state else jnp.zeros(s_scr.shape, F32)
        for u in range(cfg.units):
            pv_scr[u, 0:nhist] = (pv_ref[u * nhist:(u + 1) * nhist] if cfg.has_state
                                  else jnp.zeros((nhist, N_QKV_C), F32))

    seq_c, t_c = _row_info(cfg, (rows, 1), 0)
    live = t_c < valid
    incl_r, _, same_r = _pair_masks(cfg, rows)
    n = GDN_GROUP * rows
    incl, strict, _ = _pair_masks(cfg, n)

    def prep(u):
        x = qkv_ref[u]
        conv = cw_ref[CONV_W - 1:CONV_W, :] * x
        hist = pv_scr[u, 0:nhist]
        for s in range(1, CONV_W):
            conv = conv + cw_ref[CONV_W - 1 - s:CONV_W - s, :] * _shifted(cfg, x, hist, t_c, s)
        pv_scr[u, 0:nhist] = _new_history(cfg, x)
        qkv = _silu(conv)
        ba = ba_ref[u]
        beta_all = jnp.where(live, _sigmoid(ba[:, 0:LANES]), 0.0)
        g_all = jnp.where(live, -jnp.exp(alog_ref[...]) * _softplus(ba[:, LANES:2 * LANES] + dtb_ref[...]), 0.0)
        return qkv, beta_all, _mask_dot(incl_r, g_all), _mask_dot(same_r, g_all)

    def group(u, gi, qkv, beta_all, gc_all, gl_all):
        heads = range(gi * GDN_GROUP, (gi + 1) * GDN_GROUP)

        def stack(arr, base):
            return jnp.concatenate([arr[:, base + DK_C * h:base + DK_C * (h + 1)] for h in heads], axis=0)

        def stack_col(arr):
            return jnp.concatenate([arr[:, h:h + 1] for h in heads], axis=0)

        def unit(a):
            return a * lax.rsqrt(jnp.maximum(jnp.sum(a * a, -1, keepdims=True), 1e-12))

        qs = unit(stack(qkv, 0)) * DK_C ** -0.5
        ks = unit(stack(qkv, H_C * DK_C))
        vs = stack(qkv, 2 * H_C * DK_C)
        bs, gcs, gls = stack_col(beta_all), stack_col(gc_all), stack_col(gl_all)
        diff = gcs - gcs.T
        dmat = jnp.where(incl, jnp.exp(jnp.where(incl, diff, 0.0)), 0.0)
        kk = _bdot_nt(ks, ks)
        qk = _bdot_nt(qs, ks) * dmat
        yield
        a = jnp.where(strict, bs * kk * dmat, 0.0)
        tinv = yield from _unit_lower_inverse(-a, lb)
        egc = jnp.exp(gcs)
        uw = _bdot(tinv, jnp.concatenate([vs * bs, ks * (bs * egc)], axis=1))
        u_, w = uw[:, 0:DV_C], uw[:, DV_C:DV_C + DK_C]
        yield
        qg = qs * egc
        kd = ks * jnp.exp(gls - gcs)
        egl = jnp.exp(gls)
        v_new, o_state = [], []
        for hh, h in enumerate(heads):
            rs = slice(hh * rows, (hh + 1) * rows)
            ws, os_ = [], []
            for s in range(cfg.nseq):
                sr = _seq_rows(cfg, s, hh * rows)
                res = _bdot(jnp.concatenate([w[sr], qg[sr]], axis=0), s_scr[u * cfg.nseq + s, h])
                ws.append(res[0:lb])
                os_.append(res[lb:2 * lb])
            v_new.append(u_[rs] - jnp.concatenate(ws, axis=0))
            o_state.append(jnp.concatenate(os_, axis=0))
        yield
        v_new_s = jnp.concatenate(v_new, axis=0)
        o_s = jnp.concatenate(o_state, axis=0) + _bdot(qk, v_new_s)
        for hh, h in enumerate(heads):
            for s in range(cfg.nseq):
                sr = _seq_rows(cfg, s, hh * rows)
                r0 = hh * rows + s * lb
                s_scr[u * cfg.nseq + s, h] = (egl[r0:r0 + 1, :] * s_scr[u * cfg.nseq + s, h]
                                              + _bdot_tn(kd[sr], v_new_s[sr]))
        yield
        for hh, h in enumerate(heads):
            o = o_s[hh * rows:(hh + 1) * rows]
            o = o * lax.rsqrt(jnp.mean(o * o, -1, keepdims=True) + 1e-6) * ng_ref[...]
            o_ref[u, :, h * DV_C:(h + 1) * DV_C] = o * _silu(zg_ref[u, :, h * DV_C:(h + 1) * DV_C])

    preps = [prep(u) for u in range(cfg.units)]
    _interleave(group(u, gi, *preps[u]) for u in range(cfg.units) for gi in range(H_C // GDN_GROUP))

    @pl.when(c == cfg.nc - 1)
    def _():
        for u in range(cfg.units):
            cvout_ref[u * nhist:(u + 1) * nhist] = pv_scr[u, 0:nhist]
        sout_ref[...] = s_scr[...]


def gdn(qkv, ba, zg, conv_prev, s0, conv_w, a_log, dt_bias, norm_g, i, cfg):
    par = lambda n: ((None, 1, n), lambda b, c: (i, 0, 0))
    params = [(conv_w, (None, CONV_W, N_QKV_C), lambda b, c: (i, 0, 0)), (a_log,) + par(LANES),
              (dt_bias,) + par(LANES), (norm_g,) + par(DV_C)]
    return _seq_call(_gdn_kernel, cfg, "gdn", [qkv, ba, zg], [conv_prev], [s0], params,
                     [C_C], [N_QKV_C], [(H_C, DK_C, DV_C)],
                     [pltpu.VMEM((cfg.units * cfg.nseq, H_C, DK_C, DV_C), F32),
                      pltpu.VMEM((cfg.units, cfg.hist, N_QKV_C), F32)], i)


RWKV_VEC_ROWS = 8
RWKV_GL = RWKV_GROUP * N_A
RWKV_NG = H_A // RWKV_GROUP


def _rwkv_kernel(*refs, cfg):
    if cfg.has_state:
        (za_ref, pv_ref, s0_ref, mu_ref, vec_ref, lw_ref, ones_ref,
         o_ref, shout_ref, sout_ref, s_scr, pv_scr) = refs
    else:
        (za_ref, mu_ref, vec_ref, lw_ref, ones_ref, o_ref, shout_ref, sout_ref, s_scr, pv_scr) = refs
    c = pl.program_id(1)
    rows, lb, valid = cfg.rows, cfg.lb, cfg.valid
    n = RWKV_GROUP * rows
    head_shift = int(math.log2(N_A))
    row_shift = int(math.log2(rows))
    assert 1 << row_shift == rows

    bi, bj = _iota((RWKV_GL, RWKV_GL), 0), _iota((RWKV_GL, RWKV_GL), 1)
    blockdiag = (bi >> head_shift) == (bj >> head_shift)

    def slot(u, s, g):
        return (u * cfg.nseq + s) * RWKV_NG + g

    @pl.when(c == 0)
    def _():
        pv_scr[...] = pv_ref[...] if cfg.has_state else jnp.zeros(pv_scr.shape, F32)
        for u in range(cfg.units):
            for s in range(cfg.nseq):
                for g in range(RWKV_NG):
                    if cfg.has_state:
                        sc = s0_ref[u * cfg.nseq + s, :, g * RWKV_GL:(g + 1) * RWKV_GL]
                        s_scr[slot(u, s, g)] = jnp.where(blockdiag, _tile_rows(sc, RWKV_GROUP), 0.0)
                    else:
                        s_scr[slot(u, s, g)] = jnp.zeros((RWKV_GL, RWKV_GL), F32)

    seq_c, t_c = _row_info(cfg, (rows, 1), 0)
    live = t_c < valid
    incl_r, _, same_r = _pair_masks(cfg, rows)
    tj = _iota((rows, rows), 1) & (lb - 1)
    sum_masks = jnp.concatenate([incl_r.astype(BF16), (same_r & (tj < lb // 2)).astype(BF16), same_r.astype(BF16)],
                                axis=0)
    hi, hl = _iota((n, RWKV_GL), 0), _iota((n, RWKV_GL), 1)
    headmask = (hi >> row_shift) == (hl >> head_shift)
    incl, strict, _ = _pair_masks(cfg, n)
    spread = lambda x: jnp.where(headmask, _tile_rows(x, RWKV_GROUP), 0.0)
    w0, a0, k_k, k_a, r_k, gn_g, gn_b = (vec_ref[j:j + 1, :] for j in range(7))
    ones_pair = ones_ref[...]

    def prep(u):
        za = za_ref[u]
        za_prev = _shifted(cfg, za, pv_scr[u * cfg.hist:(u + 1) * cfg.hist], t_c, 1)
        pv_scr[u * cfg.hist:(u + 1) * cfg.hist] = _new_history(cfg, za)
        zs = za + (za_prev - za) * mu_ref[...]
        r = zs[:, 0:C_A]
        k = zs[:, C_A:2 * C_A]
        v = zs[:, 2 * C_A:3 * C_A]
        lo = zs[:, 3 * C_A:A_SHIFT_COLS]
        ll = _iota((rows, LORA_COLS), 1)
        lo = jnp.where(ll < W_LORA, jnp.tanh(lo), jnp.where(ll < W_LORA + A_LORA, lo, _sigmoid(lo)))
        w_log = -_softplus(-(w0 + _bdot(lo, lw_ref[0]))) - 0.5
        logw = -jnp.exp(w_log)
        a = _sigmoid(a0 + _bdot(lo, lw_ref[1]))
        gate = _bdot(lo, lw_ref[2])
        kk = k * k_k
        k = k * (1.0 + (a - 1.0) * k_a)
        sums = _head_sums(jnp.concatenate([kk * kk, r * k * r_k], axis=0), ones_pair)
        kk = kk * lax.rsqrt(jnp.maximum(sums[0:rows], 1e-12))
        bonus = sums[rows:2 * rows] * v
        logw = jnp.where(live, logw, 0.0)
        kk = jnp.where(live, kk, 0.0)
        k = jnp.where(live, k, 0.0)
        v = jnp.where(live, v, 0.0)
        ka = kk * a
        sums = _mask_dot(sum_masks, logw)
        cl, cm, ct = sums[0:rows], sums[rows:2 * rows], sums[2 * rows:3 * rows]
        clp = cl - logw
        e_mid = jnp.exp(cm - cl)
        e_end = jnp.exp(ct - cl)
        vecs = dict(
            b0=kk * jnp.exp(clp), r0=r * jnp.exp(cl),
            bc=kk * jnp.exp(clp - cm), rc=r * jnp.exp(cl - cm),
            ac=-ka * e_mid, kc=k * e_mid,
            ah=-ka * e_end, kh=k * e_end, wc=jnp.exp(ct), v=v)
        return vecs, bonus, gate

    def group(u, g, q, outs):
        sl = slice(g * RWKV_GL, (g + 1) * RWKV_GL)
        gram = _bdot_nt(jnp.concatenate([spread(q["bc"][:, sl]), spread(q["rc"][:, sl])], axis=0),
                        jnp.concatenate([_tile_rows(q["ac"][:, sl], RWKV_GROUP),
                                         _tile_rows(q["kc"][:, sl], RWKV_GROUP)], axis=0))
        u0, o0 = [], []
        for s in range(cfg.nseq):
            sr = _seq_rows(cfg, s)
            res = _bdot_nt(jnp.concatenate([q["b0"][sr, sl], q["r0"][sr, sl]], axis=0), s_scr[slot(u, s, g)])
            u0.append(res[0:lb])
            o0.append(res[lb:2 * lb])
        u0 = jnp.concatenate(u0, axis=0)
        o0 = jnp.concatenate(o0, axis=0)
        yield
        x1 = jnp.where(strict, gram[0:n, 0:n], 0.0)
        x2 = jnp.where(strict, gram[0:n, n:2 * n], 0.0)
        x3 = jnp.where(incl, gram[n:2 * n, 0:n], 0.0)
        x4 = jnp.where(incl, gram[n:2 * n, n:2 * n], 0.0)
        vh = spread(q["v"][:, sl])
        xv = _bdot(jnp.concatenate([x2, x4], axis=0), vh)
        rhs = spread(u0) + xv[0:n]
        tinv = yield from _unit_lower_inverse(x1, lb)
        uh = _bdot(tinv, rhs)
        yield
        oh = _bdot(x3, uh) + xv[n:2 * n]
        outs[u][g] = o0 + _sum_row_blocks(oh, RWKV_GROUP)
        uc = _sum_row_blocks(uh, RWKV_GROUP)
        for s in range(cfg.nseq):
            sr = _seq_rows(cfg, s)
            upd = _bdot_tn(jnp.concatenate([uc[sr], q["v"][sr, sl]], axis=0),
                           jnp.concatenate([q["ah"][sr, sl], q["kh"][sr, sl]], axis=0))
            s_scr[slot(u, s, g)] = (s_scr[slot(u, s, g)] * q["wc"][s * lb:s * lb + 1, sl]
                                    + jnp.where(blockdiag, upd, 0.0))
        yield

    preps = [prep(u) for u in range(cfg.units)]
    outs = [[None] * RWKV_NG for _ in range(cfg.units)]
    _interleave(group(u, g, preps[u][0], outs) for u in range(cfg.units) for g in range(RWKV_NG))
    for u in range(cfg.units):
        _, bonus, gate = preps[u]
        o = jnp.concatenate(outs[u], axis=1)
        mean = _head_sums(o, ones_pair) * (1.0 / N_A)
        oc = o - mean
        var = _head_sums(oc * oc, ones_pair) * (1.0 / N_A)
        o = oc * lax.rsqrt(var + RWKV_GN_EPS) * gn_g + gn_b
        o_ref[u] = (o + bonus) * gate

    @pl.when(c == cfg.nc - 1)
    def _():
        shout_ref[...] = pv_scr[...]
        for u in range(cfg.units):
            for s in range(cfg.nseq):
                for g in range(RWKV_NG):
                    sout_ref[u * cfg.nseq + s, :, g * RWKV_GL:(g + 1) * RWKV_GL] = _sum_row_blocks(
                        s_scr[slot(u, s, g)], RWKV_GROUP)


def rwkv(za, shift_prev, s0, mu, vecs, lora_w, ones_blk, i, cfg):
    params = [(mu, (None, 1, A_SHIFT_COLS), lambda b, c: (i, 0, 0)),
              (vecs, (None, RWKV_VEC_ROWS, C_A), lambda b, c: (i, 0, 0)),
              (lora_w, (None, 3, LORA_COLS, C_A), lambda b, c: (i, 0, 0, 0)),
              (ones_blk, (LANES, LANES), lambda b, c: (0, 0))]
    return _seq_call(_rwkv_kernel, cfg, "rwkv", [za], [shift_prev], [s0], params,
                     [C_A], [A_SHIFT_COLS], [(N_A, C_A)],
                     [pltpu.VMEM((cfg.units * cfg.nseq * RWKV_NG, RWKV_GL, RWKV_GL), F32),
                      pltpu.VMEM((cfg.units * cfg.hist, A_SHIFT_COLS), F32)], i)


def _trunk(xs, cfgs, sts, p):
    hs = list(xs)
    nseg = len(hs)
    new = [dict(shift=[], rwkv=[], ret=[], conv=[], gdn=[]) for _ in range(nseg)]
    for l in range(DEPTH):
        i = l // 2
        hs = token_block(hs, p, l, 0)
        outs = []
        if l % 2 == 0:
            zs = project(hs, p["ab_in"], i, (A_SHIFT_COLS, B_COLS))
            for k, ((za, zb), (cfg_rwkv, cfg_ret, _), st) in enumerate(zip(zs, cfgs, sts)):
                has = st is not None
                o_a, sh, s_rw = rwkv(za, st["shift"] if has else None, st["rwkv"] if has else None,
                                     p["mu"], p["rwkv_vecs"], p["lora_w"], p["ones_blk"], i, cfg_rwkv)
                o_b, s_rt = retention(zb, st["ret"] if has else None, p["ret_g"], p["ret_b"], i, cfg_ret)
                outs.append([o_a, o_b])
                new[k]["shift"].append(sh)
                new[k]["rwkv"].append(s_rw)
                new[k]["ret"].append(s_rt)
            w_out = p["ab_out"]
        else:
            zs = project(hs, p["gdn_in"], i, (N_QKV_C, 2 * LANES, C_C))
            for k, ((qkv, ba, zg), (_, _, cfg_gdn), st) in enumerate(zip(zs, cfgs, sts)):
                has = st is not None
                o, cv, s_gd = gdn(qkv, ba, zg, st["conv"] if has else None, st["gdn"] if has else None,
                                  p["conv_w"], p["a_log"], p["dt_bias"], p["norm_g"], i, cfg_gdn)
                outs.append([o])
                new[k]["conv"].append(cv)
                new[k]["gdn"].append(s_gd)
            w_out = p["gdn_out"]
        hs = token_block(hs, p, l, 1, mix=(outs, w_out, i))
    return hs, [{name: jnp.stack(v) for name, v in d.items()} for d in new]


def _lane_pad(x, n):
    return jnp.pad(x, [(0, 0)] * (x.ndim - 1) + [(0, n - x.shape[-1])])


def _history_tiles(x):
    nl, b, r, n = x.shape
    return jnp.pad(x, ((0, 0), (0, 0), (SUBLANES - r, 0), (0, 0))).reshape(nl, b * SUBLANES, n)


def kernel(x_prompt, x_sample, state_rwkv_shift, state_rwkv, state_ret, state_gdn_conv, state_gdn, ln_g, ln_b, ffn_w_gate, ffn_w_up, ffn_w_down, ab_w_in, ab_w_out, rwkv_mu, rwkv_w0, rwkv_w2, rwkv_a0, rwkv_a2, rwkv_g2, rwkv_k_k, rwkv_k_a, rwkv_r_k, rwkv_gn_g, rwkv_gn_b, ret_gn_g, ret_gn_b, gdn_w_in, gdn_conv_w, gdn_A_log, gdn_dt_bias, gdn_norm_g, gdn_w_out):
    nb, seq = x_prompt.shape[0], x_prompt.shape[1]
    ndec, dec_seq = x_sample.shape[0], x_sample.shape[1]
    nl = N_AB_LAYERS

    lora_w = jnp.zeros((nl, 3, LORA_COLS, C_A), F32)
    lora_w = (lora_w.at[:, 0, 0:W_LORA].set(rwkv_w2).at[:, 1, W_LORA:W_LORA + A_LORA].set(rwkv_a2)
              .at[:, 2, W_LORA + A_LORA:].set(rwkv_g2))
    gdn_in = jnp.concatenate([gdn_w_in[:, :, :N_QKV_C],
                              _lane_pad(gdn_w_in[:, :, N_QKV_C:N_QKV_C + H_C], LANES),
                              _lane_pad(gdn_w_in[:, :, N_QKV_C + H_C:N_QKV_C + 2 * H_C], LANES),
                              gdn_w_in[:, :, N_QKV_C + 2 * H_C:]], axis=2)
    lane_head = np.arange(LANES) // N_A
    p = dict(
        wg=ffn_w_gate.astype(BF16), wu=ffn_w_up.astype(BF16), wd=ffn_w_down.astype(BF16),
        ln_g=ln_g.reshape(DEPTH * 3, 1, D_MODEL), ln_b=ln_b.reshape(DEPTH * 3, 1, D_MODEL),
        ab_in=ab_w_in.astype(BF16), ab_out=ab_w_out.astype(BF16),
        gdn_in=gdn_in.astype(BF16), gdn_out=gdn_w_out.astype(BF16),
        mu=rwkv_mu.reshape(nl, 1, A_SHIFT_COLS),
        rwkv_vecs=jnp.stack([rwkv_w0, rwkv_a0, rwkv_k_k, rwkv_k_a, rwkv_r_k.reshape(nl, C_A), rwkv_gn_g, rwkv_gn_b,
                             jnp.zeros_like(rwkv_w0)], axis=1),
        lora_w=lora_w.astype(BF16),
        ones_blk=jnp.asarray((lane_head[:, None] == lane_head[None, :]).astype(np.float32)),
        ret_g=ret_gn_g.reshape(nl, 1, C_B), ret_b=ret_gn_b.reshape(nl, 1, C_B),
        conv_w=gdn_conv_w, a_log=_lane_pad(gdn_A_log, LANES)[:, None, :], dt_bias=_lane_pad(gdn_dt_bias, LANES)[:, None, :],
        norm_g=gdn_norm_g[:, None, :],
    )

    cfg_p = (SeqCfg(1, 64, 64, seq // 64, 0, False, UNITS_DEEP),
             SeqCfg(1, 128, 128, seq // 128, 0, False, UNITS),
             SeqCfg(1, 64, 64, seq // 64, 0, False, UNITS_DEEP))
    cfg = SeqCfg(SEQ_PER_UNIT, SAMPLE_PAD, dec_seq, 1, PAST_LEN, True, UNITS)
    xs = jnp.pad(x_sample, ((0, 0), (0, SAMPLE_PAD - dec_seq), (0, 0))).reshape(ndec * SAMPLE_PAD, D_MODEL)
    st = dict(shift=_history_tiles(state_rwkv_shift[:, :, None, :]),
              rwkv=state_rwkv.transpose(0, 1, 3, 2, 4).reshape(nl, ndec, N_A, C_A),
              ret=state_ret,
              conv=_history_tiles(state_gdn_conv),
              gdn=state_gdn)
    (hp, hs), (sp, ss) = _trunk([x_prompt.reshape(nb * seq, D_MODEL), xs], [cfg_p, (cfg, cfg, cfg)], [None, st], p)

    def unpack(s, n):
        return (s["shift"].reshape(nl, n, SUBLANES, A_SHIFT_COLS)[:, :, SUBLANES - 1],
                s["rwkv"].reshape(nl, n, N_A, H_A, N_A).transpose(0, 1, 3, 2, 4),
                s["ret"],
                s["conv"].reshape(N_C_LAYERS, n, SUBLANES, N_QKV_C)[:, :, SUBLANES - (CONV_W - 1):],
                s["gdn"])

    y_prompt = hp.reshape(nb, seq, D_MODEL)
    y_sample = hs.reshape(ndec, SAMPLE_PAD, D_MODEL)[:, :dec_seq]
    return (y_prompt, y_sample) + unpack(sp, nb) + unpack(ss, ndec)
```

```python
import functools
import math

import numpy as np
import jax
import jax.numpy as jnp
from jax import lax
from jax.experimental import pallas as pl
from jax.experimental.pallas import tpu as pltpu

F32 = jnp.float32
BF16 = jnp.bfloat16

D_MODEL = 1024
DEPTH = 4
PAST_LEN = 16384
N_AB_LAYERS = (DEPTH + 1) // 2
N_C_LAYERS = DEPTH // 2
H_A, N_A = 8, 64
C_A = H_A * N_A
W_LORA, A_LORA, G_LORA = 64, 64, 128
LORA_COLS = W_LORA + A_LORA + G_LORA
A_SHIFT_COLS = 3 * C_A + LORA_COLS
RWKV_GN_EPS = 64e-5
H_B, DK_B, DV_B = 4, 128, 128
C_B = H_B * DV_B
ROPE_THETA = 10000.0
RET_GN_EPS = 1e-5
B_COLS = 2 * H_B * DK_B + 2 * C_B
AB_COLS = A_SHIFT_COLS + B_COLS
H_C, DK_C, DV_C = 8, 128, 128
C_C = H_C * DV_C
CONV_W = 4
N_QKV_C = 2 * H_C * DK_C + C_C
D_FF = 2816
LN_EPS = 1e-5
DEEPNORM_ALPHA = (2 * DEPTH) ** 0.25

V7X_VMEM_BYTES = 64 * 1024 * 1024
LANES = 128
SUBLANES = 8
VMEM_LIMIT = V7X_VMEM_BYTES * 7 // 8

SAMPLE_PAD = SUBLANES
SEQ_PER_UNIT = 8
UNITS = 2
UNITS_DEEP = 8
ROW_TILE = 512
RWKV_GROUP = 2
GDN_GROUP = 2


class SeqCfg:
    def __init__(self, nseq, lb, valid, nc, pos0, has_state, units):
        self.nseq, self.lb, self.valid, self.nc, self.pos0, self.has_state = nseq, lb, valid, nc, pos0, has_state
        self.units = units
        self.rows = nseq * lb
        self.shift = int(math.log2(lb))
        self.hist = nseq * SUBLANES
        assert 1 << self.shift == lb and lb >= SUBLANES and CONV_W - 1 <= valid <= lb


def _cparams(ngrid):
    return pltpu.CompilerParams(dimension_semantics=("arbitrary",) * ngrid, vmem_limit_bytes=VMEM_LIMIT)


def _bdot(a, b):
    return jnp.dot(a.astype(BF16), b.astype(BF16), preferred_element_type=F32)


def _bdot_nt(a, b):
    return lax.dot_general(a.astype(BF16), b.astype(BF16), (((1,), (1,)), ((), ())), preferred_element_type=F32)


def _bdot_tn(a, b):
    return lax.dot_general(a.astype(BF16), b.astype(BF16), (((0,), (0,)), ((), ())), preferred_element_type=F32)


def _split3(x):
    x1 = x.astype(BF16)
    r1 = x - x1.astype(F32)
    x2 = r1.astype(BF16)
    return x1, x2, (r1 - x2.astype(F32)).astype(BF16)


def _mask_dot(m, x):
    mb = m.astype(BF16)
    x1, x2, x3 = _split3(x)
    return (jnp.dot(mb, x1, preferred_element_type=F32) + jnp.dot(mb, x2, preferred_element_type=F32)
            + jnp.dot(mb, x3, preferred_element_type=F32))


def _sigmoid(x):
    return 1.0 / (1.0 + jnp.exp(-x))


def _silu(x):
    return x * _sigmoid(x)


def _softplus(x):
    return jnp.maximum(x, 0.0) + jnp.log(1.0 + jnp.exp(-jnp.abs(x)))


def _layer_norm(y, g, b):
    yc = y - jnp.mean(y, -1, keepdims=True)
    var = jnp.mean(yc * yc, -1, keepdims=True)
    return yc * lax.rsqrt(var + LN_EPS) * g + b


def _iota(shape, dim):
    return lax.broadcasted_iota(jnp.int32, shape, dim)


def _tile_rows(x, n):
    return jnp.concatenate([x] * n, axis=0)


def _sum_row_blocks(x, n):
    r = x.shape[0] // n
    out = x[0:r]
    for i in range(1, n):
        out = out + x[i * r:(i + 1) * r]
    return out


def _unit_lower_inverse(x, lb):
    n = x.shape[0]
    eye = (_iota((n, n), 0) == _iota((n, n), 1)).astype(F32)
    p = eye + x
    rounds = int(math.log2(lb)) - 1
    y = _bdot(x, x)
    yield
    for r in range(rounds):
        if r == rounds - 1:
            p = p + _bdot(p, y)
        else:
            z = _bdot(jnp.concatenate([p, y], axis=0), y)
            p = p + z[0:n]
            y = z[n:2 * n]
        yield
    return p


def _head_sums(x, ones_pair):
    rows = x.shape[0]
    nblk = x.shape[1] // LANES
    st = jnp.concatenate([x[:, j * LANES:(j + 1) * LANES] for j in range(nblk)], axis=0)
    s = _bdot(st, ones_pair)
    return jnp.concatenate([s[j * rows:(j + 1) * rows] for j in range(nblk)], axis=1)


def _interleave(gens):
    gens = list(gens)
    while gens:
        for g in list(gens):
            try:
                next(g)
            except StopIteration:
                gens.remove(g)


def _segment_steps(seg_tiles, body):
    r = pl.program_id(0)
    off = 0
    for k, n in enumerate(seg_tiles):
        if len(seg_tiles) == 1:
            body(k)
        else:
            pl.when((r >= off) & (r < off + n))(functools.partial(body, k))
        off += n


def _segment_rows(seg_tiles, k, width):
    off = sum(seg_tiles[:k])
    last = seg_tiles[k] - 1
    return pl.BlockSpec((ROW_TILE, width), lambda r: (jnp.clip(r - off, 0, last), 0))


def _token_kernel(*refs, n_mix, seg_tiles):
    nseg = len(seg_tiles)
    it = iter(refs)
    seg_in = [[next(it) for _ in range(1 + n_mix)] for _ in range(nseg)]
    if n_mix:
        w_ref, g1_ref, b1_ref = next(it), next(it), next(it)
    wg_ref, wu_ref, wd_ref, g_ref, b_ref = (next(it) for _ in range(5))
    out_refs = [next(it) for _ in range(nseg)]

    def body(k):
        x = seg_in[k][0][...]
        if n_mix:
            acc = DEEPNORM_ALPHA * x
            off = 0
            for o_ref in seg_in[k][1:]:
                n = o_ref.shape[1]
                acc = acc + jnp.dot(o_ref[...].astype(BF16), w_ref[off:off + n, :], preferred_element_type=F32)
                off += n
            x = _layer_norm(acc, g1_ref[...], b1_ref[...])
        xb = x.astype(BF16)
        gate = jnp.dot(xb, wg_ref[...], preferred_element_type=F32)
        up = jnp.dot(xb, wu_ref[...], preferred_element_type=F32)
        act = (_silu(gate) * up).astype(BF16)
        f = jnp.dot(act, wd_ref[...], preferred_element_type=F32)
        out_refs[k][...] = _layer_norm(DEEPNORM_ALPHA * x + 0.5 * f, g_ref[...], b_ref[...])

    _segment_steps(seg_tiles, body)


def _resident(block_shape, index_map):
    return pl.BlockSpec(block_shape, index_map, pipeline_mode=pl.Buffered(1))


def token_block(hs, p, l, j, mix=None):
    seg_tiles = tuple(h.shape[0] // ROW_TILE for h in hs)
    ln = lambda k: pl.BlockSpec((None, 1, D_MODEL), lambda r: (3 * l + k, 0, 0))
    args, in_specs = [], []
    for k, h in enumerate(hs):
        seg = [h] + (list(mix[0][k]) if mix else [])
        args += seg
        in_specs += [_segment_rows(seg_tiles, k, a.shape[1]) for a in seg]
    if mix:
        _, w_out, i = mix
        args += [w_out, p["ln_g"], p["ln_b"]]
        in_specs += [_resident((None, D_MODEL, D_MODEL), lambda r: (i, 0, 0)), ln(1), ln(1)]
    args += [p["wg"], p["wu"], p["wd"], p["ln_g"], p["ln_b"]]
    in_specs += [_resident((None, None, D_MODEL, D_FF), lambda r: (l, j, 0, 0)),
                 _resident((None, None, D_MODEL, D_FF), lambda r: (l, j, 0, 0)),
                 _resident((None, None, D_FF, D_MODEL), lambda r: (l, j, 0, 0)), ln(2 * j), ln(2 * j)]
    return pl.pallas_call(
        functools.partial(_token_kernel, n_mix=len(mix[0][0]) if mix else 0, seg_tiles=seg_tiles),
        grid=(sum(seg_tiles),),
        in_specs=in_specs,
        out_specs=[_segment_rows(seg_tiles, k, D_MODEL) for k in range(len(hs))],
        out_shape=[jax.ShapeDtypeStruct(h.shape, F32) for h in hs],
        compiler_params=_cparams(1),
        name="token_block",
    )(*args)


def _proj_kernel(*refs, seg_tiles, n_out):
    nseg = len(seg_tiles)
    h_refs, w_ref, o_refs = refs[:nseg], refs[nseg], refs[nseg + 1:]

    def body(k):
        xb = h_refs[k][...].astype(BF16)
        off = 0
        for o_ref in o_refs[k * n_out:(k + 1) * n_out]:
            n = o_ref.shape[1]
            o_ref[...] = jnp.dot(xb, w_ref[:, off:off + n], preferred_element_type=F32)
            off += n

    _segment_steps(seg_tiles, body)


def project(hs, w, i, widths):
    seg_tiles = tuple(h.shape[0] // ROW_TILE for h in hs)
    ncols = w.shape[2]
    assert sum(widths) == ncols
    outs = pl.pallas_call(
        functools.partial(_proj_kernel, seg_tiles=seg_tiles, n_out=len(widths)),
        grid=(sum(seg_tiles),),
        in_specs=[_segment_rows(seg_tiles, k, D_MODEL) for k in range(len(hs))]
        + [_resident((None, D_MODEL, ncols), lambda r: (i, 0, 0))],
        out_specs=[_segment_rows(seg_tiles, k, n) for k in range(len(hs)) for n in widths],
        out_shape=[jax.ShapeDtypeStruct((h.shape[0], n), F32) for h in hs for n in widths],
        compiler_params=_cparams(1),
        name="project",
    )(*hs, w)
    return [outs[k * len(widths):(k + 1) * len(widths)] for k in range(len(hs))]


def _row_info(cfg, shape, dim):
    r = _iota(shape, dim)
    return r >> cfg.shift, r & (cfg.lb - 1)


def _seq_rows(cfg, s, base=0):
    return slice(base + s * cfg.lb, base + (s + 1) * cfg.lb)


def _pair_masks(cfg, n):
    ri, ci = _iota((n, n), 0), _iota((n, n), 1)
    same = (ri >> cfg.shift) == (ci >> cfg.shift)
    ti, tj = ri & (cfg.lb - 1), ci & (cfg.lb - 1)
    return same & (tj <= ti), same & (tj < ti), same


def _shifted(cfg, x, hist, t_c, s):
    rolled = pltpu.roll(x, s, 0)
    if cfg.nseq == 1:
        head = jnp.where(t_c[0:SUBLANES] >= s, rolled[0:SUBLANES], pltpu.roll(hist, s, 0))
        return jnp.concatenate([head, rolled[SUBLANES:]], axis=0)
    assert cfg.hist == cfg.rows
    return jnp.where(t_c >= s, rolled, pltpu.roll(hist, cfg.rows - SUBLANES + s, 0))


def _new_history(cfg, x):
    rolled = x if cfg.lb == cfg.valid else pltpu.roll(x, cfg.lb - cfg.valid, 0)
    return jnp.concatenate([rolled[(j + 1) * cfg.lb - SUBLANES:(j + 1) * cfg.lb] for j in range(cfg.nseq)], axis=0)


def _seq_call(kernel, cfg, name, rows_in, hist_in, state_in, params, rows_out, hist_out, state_out, scratch, i):
    t = rows_in[0].shape[0]
    nbt = t // (cfg.rows * cfg.nc)
    assert nbt % cfg.units == 0
    grid = (nbt // cfg.units, cfg.nc)

    def zeros(n):
        return (0,) * n

    row_spec = lambda n: pl.BlockSpec((cfg.units, None, cfg.rows, n), lambda b, c: (b, c, 0, 0))
    args = [a.reshape(nbt, cfg.nc, cfg.rows, a.shape[1]) for a in rows_in]
    in_specs = [row_spec(a.shape[1]) for a in rows_in]
    if cfg.has_state:
        for a in hist_in:
            args.append(a)
            in_specs.append(pl.BlockSpec((None, cfg.units * cfg.hist, a.shape[2]), lambda b, c: (i, b, 0)))
        for a in state_in:
            nd = a.ndim - 2
            args.append(a)
            in_specs.append(pl.BlockSpec((None, cfg.units * cfg.nseq) + a.shape[2:],
                                         lambda b, c, nd=nd: (i, b) + zeros(nd)))
    for a, bs, im in params:
        args.append(a)
        in_specs.append(pl.BlockSpec(bs, im))
    out_specs = [row_spec(n) for n in rows_out]
    out_shape = [jax.ShapeDtypeStruct((nbt, cfg.nc, cfg.rows, n), F32) for n in rows_out]
    for n in hist_out:
        out_specs.append(pl.BlockSpec((cfg.units * cfg.hist, n), lambda b, c: (b, 0)))
        out_shape.append(jax.ShapeDtypeStruct((nbt * cfg.hist, n), F32))
    for shp in state_out:
        out_specs.append(pl.BlockSpec((cfg.units * cfg.nseq,) + shp, lambda b, c, nd=len(shp): (b,) + zeros(nd)))
        out_shape.append(jax.ShapeDtypeStruct((nbt * cfg.nseq,) + shp, F32))
    outs = pl.pallas_call(
        functools.partial(kernel, cfg=cfg),
        grid=grid, in_specs=in_specs, out_specs=out_specs, out_shape=out_shape,
        scratch_shapes=scratch, compiler_params=_cparams(2), name=name,
    )(*args)
    return [o.reshape(t, o.shape[-1]) for o in outs[:len(rows_out)]] + list(outs[len(rows_out):])


def _ret_kernel(*refs, cfg):
    if cfg.has_state:
        zb_ref, s0_ref, g_ref, b_ref, o_ref, sout_ref, s_scr = refs
    else:
        zb_ref, g_ref, b_ref, o_ref, sout_ref, s_scr = refs
    c = pl.program_id(1)
    rows = cfg.rows

    @pl.when(c == 0)
    def _():
        s_scr[...] = s0_ref[...] if cfg.has_state else jnp.zeros(s_scr.shape, F32)

    seq_c, t_c = _row_info(cfg, (rows, 1), 0)
    live = t_c < cfg.valid
    incl, _, _ = _pair_masks(cfg, rows)
    ti = (_iota((rows, rows), 0) & (cfg.lb - 1))
    tj = (_iota((rows, rows), 1) & (cfg.lb - 1))
    steps_i = jnp.minimum(ti + 1, cfg.valid).astype(F32)
    steps_j = jnp.minimum(tj + 1, cfg.valid).astype(F32)
    steps_c = jnp.minimum(t_c + 1, cfg.valid).astype(F32)

    lane = _iota((rows, LANES), 1)
    half = DK_B // 2
    inv_freq = 1.0 / jnp.power(ROPE_THETA, (lane & (half - 1)).astype(F32) / (half - 1.0))
    pos = (cfg.pos0 + c * cfg.lb + t_c).astype(F32)
    ang = pos * inv_freq
    cos = jnp.cos(ang)
    sin = jnp.sin(ang)
    sin = jnp.where(lane < half, -sin, sin)

    def rot(x):
        return x * cos + pltpu.roll(x, half, 1) * sin

    def head(u, h):
        sl = slice(h * DK_B, (h + 1) * DK_B)
        q = rot(zb_ref[u, :, sl])
        k = rot(zb_ref[u, :, H_B * DK_B + h * DK_B:H_B * DK_B + (h + 1) * DK_B]) * DK_B ** -0.5
        v = zb_ref[u, :, 2 * H_B * DK_B + h * DV_B:2 * H_B * DK_B + (h + 1) * DV_B]
        gate = zb_ref[u, :, 2 * H_B * DK_B + C_B + h * DV_B:2 * H_B * DK_B + C_B + (h + 1) * DV_B]
        k = jnp.where(live, k, 0.0)
        v = jnp.where(live, v, 0.0)
        lam = math.log(1.0 - 2.0 ** (-5.0 - h))
        dmat = jnp.where(incl, jnp.exp(lam * (steps_i - steps_j)), 0.0)
        sc = _bdot_nt(q, k) * dmat
        yield
        o = _bdot(sc, v)
        kz = k * jnp.exp(lam * (cfg.valid - steps_c))
        xi = jnp.exp(lam * steps_c)
        cross = []
        for s in range(cfg.nseq):
            rs = _seq_rows(cfg, s)
            st = s_scr[u * cfg.nseq + s, h]
            cross.append(_bdot(q[rs], st))
            s_scr[u * cfg.nseq + s, h] = math.exp(lam * cfg.valid) * st + _bdot_tn(kz[rs], v[rs])
        o = o + jnp.concatenate(cross, axis=0) * xi
        yield
        oc = o - jnp.mean(o, -1, keepdims=True)
        var = jnp.mean(oc * oc, -1, keepdims=True)
        o = oc * lax.rsqrt(var + RET_GN_EPS) * g_ref[:, sl] + b_ref[:, sl]
        o_ref[u, :, sl] = o * _silu(gate)

    _interleave(head(u, h) for u in range(cfg.units) for h in range(H_B))

    @pl.when(c == cfg.nc - 1)
    def _():
        sout_ref[...] = s_scr[...]


def retention(zb, s0, gn_g, gn_b, i, cfg):
    par = ((None, 1, C_B), lambda b, c: (i, 0, 0))
    return _seq_call(_ret_kernel, cfg, "retention", [zb], [], [s0], [(gn_g,) + par, (gn_b,) + par],
                     [C_B], [], [(H_B, DK_B, DV_B)],
                     [pltpu.VMEM((cfg.units * cfg.nseq, H_B, DK_B, DV_B), F32)], i)


def _gdn_kernel(*refs, cfg):
    if cfg.has_state:
        (qkv_ref, ba_ref, zg_ref, pv_ref, s0_ref, cw_ref, alog_ref, dtb_ref, ng_ref,
         o_ref, cvout_ref, sout_ref, s_scr, pv_scr) = refs
    else:
        (qkv_ref, ba_ref, zg_ref, cw_ref, alog_ref, dtb_ref, ng_ref,
         o_ref, cvout_ref, sout_ref, s_scr, pv_scr) = refs
    c = pl.program_id(1)
    rows, lb, valid, nhist = cfg.rows, cfg.lb, cfg.valid, cfg.hist

    @pl.when(c == 0)
    def _():
        s_scr[...] = s0_ref[...] if cfg.has_state else jnp.zeros(s_scr.shape, F32)
        for u in range(cfg.units):
            pv_scr[u, 0:nhist] = (pv_ref[u * nhist:(u + 1) * nhist] if cfg.has_state
                                  else jnp.zeros((nhist, N_QKV_C), F32))

    seq_c, t_c = _row_info(cfg, (rows, 1), 0)
    live = t_c < valid
    incl_r, _, same_r = _pair_masks(cfg, rows)
    n = GDN_GROUP * rows
    incl, strict, _ = _pair_masks(cfg, n)

    def prep(u):
        x = qkv_ref[u]
        conv = cw_ref[CONV_W - 1:CONV_W, :] * x
        hist = pv_scr[u, 0:nhist]
        for s in range(1, CONV_W):
            conv = conv + cw_ref[CONV_W - 1 - s:CONV_W - s, :] * _shifted(cfg, x, hist, t_c, s)
        pv_scr[u, 0:nhist] = _new_history(cfg, x)
        qkv = _silu(conv)
        ba = ba_ref[u]
        beta_all = jnp.where(live, _sigmoid(ba[:, 0:LANES]), 0.0)
        g_all = jnp.where(live, -jnp.exp(alog_ref[...]) * _softplus(ba[:, LANES:2 * LANES] + dtb_ref[...]), 0.0)
        return qkv, beta_all, _mask_dot(incl_r, g_all), _mask_dot(same_r, g_all)

    def group(u, gi, qkv, beta_all, gc_all, gl_all):
        heads = range(gi * GDN_GROUP, (gi + 1) * GDN_GROUP)

        def stack(arr, base):
            return jnp.concatenate([arr[:, base + DK_C * h:base + DK_C * (h + 1)] for h in heads], axis=0)

        def stack_col(arr):
            return jnp.concatenate([arr[:, h:h + 1] for h in heads], axis=0)

        def unit(a):
            return a * lax.rsqrt(jnp.maximum(jnp.sum(a * a, -1, keepdims=True), 1e-12))

        qs = unit(stack(qkv, 0)) * DK_C ** -0.5
        ks = unit(stack(qkv, H_C * DK_C))
        vs = stack(qkv, 2 * H_C * DK_C)
        bs, gcs, gls = stack_col(beta_all), stack_col(gc_all), stack_col(gl_all)
        diff = gcs - gcs.T
        dmat = jnp.where(incl, jnp.exp(jnp.where(incl, diff, 0.0)), 0.0)
        kk = _bdot_nt(ks, ks)
        qk = _bdot_nt(qs, ks) * dmat
        yield
        a = jnp.where(strict, bs * kk * dmat, 0.0)
        tinv = yield from _unit_lower_inverse(-a, lb)
        egc = jnp.exp(gcs)
        uw = _bdot(tinv, jnp.concatenate([vs * bs, ks * (bs * egc)], axis=1))
        u_, w = uw[:, 0:DV_C], uw[:, DV_C:DV_C + DK_C]
        yield
        qg = qs * egc
        kd = ks * jnp.exp(gls - gcs)
        egl = jnp.exp(gls)
        v_new, o_state = [], []
        for hh, h in enumerate(heads):
            rs = slice(hh * rows, (hh + 1) * rows)
            ws, os_ = [], []
            for s in range(cfg.nseq):
                sr = _seq_rows(cfg, s, hh * rows)
                res = _bdot(jnp.concatenate([w[sr], qg[sr]], axis=0), s_scr[u * cfg.nseq + s, h])
                ws.append(res[0:lb])
                os_.append(res[lb:2 * lb])
            v_new.append(u_[rs] - jnp.concatenate(ws, axis=0))
            o_state.append(jnp.concatenate(os_, axis=0))
        yield
        v_new_s = jnp.concatenate(v_new, axis=0)
        o_s = jnp.concatenate(o_state, axis=0) + _bdot(qk, v_new_s)
        for hh, h in enumerate(heads):
            for s in range(cfg.nseq):
                sr = _seq_rows(cfg, s, hh * rows)
                r0 = hh * rows + s * lb
                s_scr[u * cfg.nseq + s, h] = (egl[r0:r0 + 1, :] * s_scr[u * cfg.nseq + s, h]
                                              + _bdot_tn(kd[sr], v_new_s[sr]))
        yield
        for hh, h in enumerate(heads):
            o = o_s[hh * rows:(hh + 1) * rows]
            o = o * lax.rsqrt(jnp.mean(o * o, -1, keepdims=True) + 1e-6) * ng_ref[...]
            o_ref[u, :, h * DV_C:(h + 1) * DV_C] = o * _silu(zg_ref[u, :, h * DV_C:(h + 1) * DV_C])

    preps = [prep(u) for u in range(cfg.units)]
    _interleave(group(u, gi, *preps[u]) for u in range(cfg.units) for gi in range(H_C // GDN_GROUP))

    @pl.when(c == cfg.nc - 1)
    def _():
        for u in range(cfg.units):
            cvout_ref[u * nhist:(u + 1) * nhist] = pv_scr[u, 0:nhist]
        sout_ref[...] = s_scr[...]


def gdn(qkv, ba, zg, conv_prev, s0, conv_w, a_log, dt_bias, norm_g, i, cfg):
    par = lambda n: ((None, 1, n), lambda b, c: (i, 0, 0))
    params = [(conv_w, (None, CONV_W, N_QKV_C), lambda b, c: (i, 0, 0)), (a_log,) + par(LANES),
              (dt_bias,) + par(LANES), (norm_g,) + par(DV_C)]
    return _seq_call(_gdn_kernel, cfg, "gdn", [qkv, ba, zg], [conv_prev], [s0], params,
                     [C_C], [N_QKV_C], [(H_C, DK_C, DV_C)],
                     [pltpu.VMEM((cfg.units * cfg.nseq, H_C, DK_C, DV_C), F32),
                      pltpu.VMEM((cfg.units, cfg.hist, N_QKV_C), F32)], i)


RWKV_VEC_ROWS = 8
RWKV_GL = RWKV_GROUP * N_A
RWKV_NG = H_A // RWKV_GROUP


def _rwkv_kernel(*refs, cfg):
    if cfg.has_state:
        (za_ref, pv_ref, s0_ref, mu_ref, vec_ref, lw_ref, ones_ref,
         o_ref, shout_ref, sout_ref, s_scr, pv_scr) = refs
    else:
        (za_ref, mu_ref, vec_ref, lw_ref, ones_ref, o_ref, shout_ref, sout_ref, s_scr, pv_scr) = refs
    c = pl.program_id(1)
    rows, lb, valid = cfg.rows, cfg.lb, cfg.valid
    n = RWKV_GROUP * rows
    head_shift = int(math.log2(N_A))
    row_shift = int(math.log2(rows))
    assert 1 << row_shift == rows

    bi, bj = _iota((RWKV_GL, RWKV_GL), 0), _iota((RWKV_GL, RWKV_GL), 1)
    blockdiag = (bi >> head_shift) == (bj >> head_shift)

    def slot(u, s, g):
        return (u * cfg.nseq + s) * RWKV_NG + g

    @pl.when(c == 0)
    def _():
        pv_scr[...] = pv_ref[...] if cfg.has_state else jnp.zeros(pv_scr.shape, F32)
        for u in range(cfg.units):
            for s in range(cfg.nseq):
                for g in range(RWKV_NG):
                    if cfg.has_state:
                        sc = s0_ref[u * cfg.nseq + s, :, g * RWKV_GL:(g + 1) * RWKV_GL]
                        s_scr[slot(u, s, g)] = jnp.where(blockdiag, _tile_rows(sc, RWKV_GROUP), 0.0)
                    else:
                        s_scr[slot(u, s, g)] = jnp.zeros((RWKV_GL, RWKV_GL), F32)

    seq_c, t_c = _row_info(cfg, (rows, 1), 0)
    live = t_c < valid
    incl_r, _, same_r = _pair_masks(cfg, rows)
    tj = _iota((rows, rows), 1) & (lb - 1)
    sum_masks = jnp.concatenate([incl_r.astype(BF16), (same_r & (tj < lb // 2)).astype(BF16), same_r.astype(BF16)],
                                axis=0)
    hi, hl = _iota((n, RWKV_GL), 0), _iota((n, RWKV_GL), 1)
    headmask = (hi >> row_shift) == (hl >> head_shift)
    incl, strict, _ = _pair_masks(cfg, n)
    spread = lambda x: jnp.where(headmask, _tile_rows(x, RWKV_GROUP), 0.0)
    w0, a0, k_k, k_a, r_k, gn_g, gn_b = (vec_ref[j:j + 1, :] for j in range(7))
    ones_pair = ones_ref[...]

    def prep(u):
        za = za_ref[u]
        za_prev = _shifted(cfg, za, pv_scr[u * cfg.hist:(u + 1) * cfg.hist], t_c, 1)
        pv_scr[u * cfg.hist:(u + 1) * cfg.hist] = _new_history(cfg, za)
        zs = za + (za_prev - za) * mu_ref[...]
        r = zs[:, 0:C_A]
        k = zs[:, C_A:2 * C_A]
        v = zs[:, 2 * C_A:3 * C_A]
        lo = zs[:, 3 * C_A:A_SHIFT_COLS]
        ll = _iota((rows, LORA_COLS), 1)
        lo = jnp.where(ll < W_LORA, jnp.tanh(lo), jnp.where(ll < W_LORA + A_LORA, lo, _sigmoid(lo)))
        w_log = -_softplus(-(w0 + _bdot(lo, lw_ref[0]))) - 0.5
        logw = -jnp.exp(w_log)
        a = _sigmoid(a0 + _bdot(lo, lw_ref[1]))
        gate = _bdot(lo, lw_ref[2])
        kk = k * k_k
        k = k * (1.0 + (a - 1.0) * k_a)
        sums = _head_sums(jnp.concatenate([kk * kk, r * k * r_k], axis=0), ones_pair)
        kk = kk * lax.rsqrt(jnp.maximum(sums[0:rows], 1e-12))
        bonus = sums[rows:2 * rows] * v
        logw = jnp.where(live, logw, 0.0)
        kk = jnp.where(live, kk, 0.0)
        k = jnp.where(live, k, 0.0)
        v = jnp.where(live, v, 0.0)
        ka = kk * a
        sums = _mask_dot(sum_masks, logw)
        cl, cm, ct = sums[0:rows], sums[rows:2 * rows], sums[2 * rows:3 * rows]
        clp = cl - logw
        e_mid = jnp.exp(cm - cl)
        e_end = jnp.exp(ct - cl)
        vecs = dict(
            b0=kk * jnp.exp(clp), r0=r * jnp.exp(cl),
            bc=kk * jnp.exp(clp - cm), rc=r * jnp.exp(cl - cm),
            ac=-ka * e_mid, kc=k * e_mid,
            ah=-ka * e_end, kh=k * e_end, wc=jnp.exp(ct), v=v)
        return vecs, bonus, gate

    def group(u, g, q, outs):
        sl = slice(g * RWKV_GL, (g + 1) * RWKV_GL)
        gram = _bdot_nt(jnp.concatenate([spread(q["bc"][:, sl]), spread(q["rc"][:, sl])], axis=0),
                        jnp.concatenate([_tile_rows(q["ac"][:, sl], RWKV_GROUP),
                                         _tile_rows(q["kc"][:, sl], RWKV_GROUP)], axis=0))
        u0, o0 = [], []
        for s in range(cfg.nseq):
            sr = _seq_rows(cfg, s)
            res = _bdot_nt(jnp.concatenate([q["b0"][sr, sl], q["r0"][sr, sl]], axis=0), s_scr[slot(u, s, g)])
            u0.append(res[0:lb])
            o0.append(res[lb:2 * lb])
        u0 = jnp.concatenate(u0, axis=0)
        o0 = jnp.concatenate(o0, axis=0)
        yield
        x1 = jnp.where(strict, gram[0:n, 0:n], 0.0)
        x2 = jnp.where(strict, gram[0:n, n:2 * n], 0.0)
        x3 = jnp.where(incl, gram[n:2 * n, 0:n], 0.0)
        x4 = jnp.where(incl, gram[n:2 * n, n:2 * n], 0.0)
        vh = spread(q["v"][:, sl])
        xv = _bdot(jnp.concatenate([x2, x4], axis=0), vh)
        rhs = spread(u0) + xv[0:n]
        tinv = yield from _unit_lower_inverse(x1, lb)
        uh = _bdot(tinv, rhs)
        yield
        oh = _bdot(x3, uh) + xv[n:2 * n]
        outs[u][g] = o0 + _sum_row_blocks(oh, RWKV_GROUP)
        uc = _sum_row_blocks(uh, RWKV_GROUP)
        for s in range(cfg.nseq):
            sr = _seq_rows(cfg, s)
            upd = _bdot_tn(jnp.concatenate([uc[sr], q["v"][sr, sl]], axis=0),
                           jnp.concatenate([q["ah"][sr, sl], q["kh"][sr, sl]], axis=0))
            s_scr[slot(u, s, g)] = (s_scr[slot(u, s, g)] * q["wc"][s * lb:s * lb + 1, sl]
                                    + jnp.where(blockdiag, upd, 0.0))
        yield

    preps = [prep(u) for u in range(cfg.units)]
    outs = [[None] * RWKV_NG for _ in range(cfg.units)]
    _interleave(group(u, g, preps[u][0], outs) for u in range(cfg.units) for g in range(RWKV_NG))
    for u in range(cfg.units):
        _, bonus, gate = preps[u]
        o = jnp.concatenate(outs[u], axis=1)
        mean = _head_sums(o, ones_pair) * (1.0 / N_A)
        oc = o - mean
        var = _head_sums(oc * oc, ones_pair) * (1.0 / N_A)
        o = oc * lax.rsqrt(var + RWKV_GN_EPS) * gn_g + gn_b
        o_ref[u] = (o + bonus) * gate

    @pl.when(c == cfg.nc - 1)
    def _():
        shout_ref[...] = pv_scr[...]
        for u in range(cfg.units):
            for s in range(cfg.nseq):
                for g in range(RWKV_NG):
                    sout_ref[u * cfg.nseq + s, :, g * RWKV_GL:(g + 1) * RWKV_GL] = _sum_row_blocks(
                        s_scr[slot(u, s, g)], RWKV_GROUP)


def rwkv(za, shift_prev, s0, mu, vecs, lora_w, ones_blk, i, cfg):
    params = [(mu, (None, 1, A_SHIFT_COLS), lambda b, c: (i, 0, 0)),
              (vecs, (None, RWKV_VEC_ROWS, C_A), lambda b, c: (i, 0, 0)),
              (lora_w, (None, 3, LORA_COLS, C_A), lambda b, c: (i, 0, 0, 0)),
              (ones_blk, (LANES, LANES), lambda b, c: (0, 0))]
    return _seq_call(_rwkv_kernel, cfg, "rwkv", [za], [shift_prev], [s0], params,
                     [C_A], [A_SHIFT_COLS], [(N_A, C_A)],
                     [pltpu.VMEM((cfg.units * cfg.nseq * RWKV_NG, RWKV_GL, RWKV_GL), F32),
                      pltpu.VMEM((cfg.units * cfg.hist, A_SHIFT_COLS), F32)], i)


def _trunk(xs, cfgs, sts, p):
    hs = list(xs)
    nseg = len(hs)
    new = [dict(shift=[], rwkv=[], ret=[], conv=[], gdn=[]) for _ in range(nseg)]
    for l in range(DEPTH):
        i = l // 2
        hs = token_block(hs, p, l, 0)
        outs = []
        if l % 2 == 0:
            zs = project(hs, p["ab_in"], i, (A_SHIFT_COLS, B_COLS))
            for k, ((za, zb), (cfg_rwkv, cfg_ret, _), st) in enumerate(zip(zs, cfgs, sts)):
                has = st is not None
                o_a, sh, s_rw = rwkv(za, st["shift"] if has else None, st["rwkv"] if has else None,
                                     p["mu"], p["rwkv_vecs"], p["lora_w"], p["ones_blk"], i, cfg_rwkv)
                o_b, s_rt = retention(zb, st["ret"] if has else None, p["ret_g"], p["ret_b"], i, cfg_ret)
                outs.append([o_a, o_b])
                new[k]["shift"].append(sh)
                new[k]["rwkv"].append(s_rw)
                new[k]["ret"].append(s_rt)
            w_out = p["ab_out"]
        else:
            zs = project(hs, p["gdn_in"], i, (N_QKV_C, 2 * LANES, C_C))
            for k, ((qkv, ba, zg), (_, _, cfg_gdn), st) in enumerate(zip(zs, cfgs, sts)):
                has = st is not None
                o, cv, s_gd = gdn(qkv, ba, zg, st["conv"] if has else None, st["gdn"] if has else None,
                                  p["conv_w"], p["a_log"], p["dt_bias"], p["norm_g"], i, cfg_gdn)
                outs.append([o])
                new[k]["conv"].append(cv)
                new[k]["gdn"].append(s_gd)
            w_out = p["gdn_out"]
        hs = token_block(hs, p, l, 1, mix=(outs, w_out, i))
    return hs, [{name: jnp.stack(v) for name, v in d.items()} for d in new]


def _lane_pad(x, n):
    return jnp.pad(x, [(0, 0)] * (x.ndim - 1) + [(0, n - x.shape[-1])])


def _history_tiles(x):
    nl, b, r, n = x.shape
    return jnp.pad(x, ((0, 0), (0, 0), (SUBLANES - r, 0), (0, 0))).reshape(nl, b * SUBLANES, n)


def kernel(x_prompt, x_sample, state_rwkv_shift, state_rwkv, state_ret, state_gdn_conv, state_gdn, ln_g, ln_b, ffn_w_gate, ffn_w_up, ffn_w_down, ab_w_in, ab_w_out, rwkv_mu, rwkv_w0, rwkv_w2, rwkv_a0, rwkv_a2, rwkv_g2, rwkv_k_k, rwkv_k_a, rwkv_r_k, rwkv_gn_g, rwkv_gn_b, ret_gn_g, ret_gn_b, gdn_w_in, gdn_conv_w, gdn_A_log, gdn_dt_bias, gdn_norm_g, gdn_w_out):
    nb, seq = x_prompt.shape[0], x_prompt.shape[1]
    ndec, dec_seq = x_sample.shape[0], x_sample.shape[1]
    nl = N_AB_LAYERS

    lora_w = jnp.zeros((nl, 3, LORA_COLS, C_A), F32)
    lora_w = (lora_w.at[:, 0, 0:W_LORA].set(rwkv_w2).at[:, 1, W_LORA:W_LORA + A_LORA].set(rwkv_a2)
              .at[:, 2, W_LORA + A_LORA:].set(rwkv_g2))
    gdn_in = jnp.concatenate([gdn_w_in[:, :, :N_QKV_C],
                              _lane_pad(gdn_w_in[:, :, N_QKV_C:N_QKV_C + H_C], LANES),
                              _lane_pad(gdn_w_in[:, :, N_QKV_C + H_C:N_QKV_C + 2 * H_C], LANES),
                              gdn_w_in[:, :, N_QKV_C + 2 * H_C:]], axis=2)
    lane_head = np.arange(LANES) // N_A
    p = dict(
        wg=ffn_w_gate.astype(BF16), wu=ffn_w_up.astype(BF16), wd=ffn_w_down.astype(BF16),
        ln_g=ln_g.reshape(DEPTH * 3, 1, D_MODEL), ln_b=ln_b.reshape(DEPTH * 3, 1, D_MODEL),
        ab_in=ab_w_in.astype(BF16), ab_out=ab_w_out.astype(BF16),
        gdn_in=gdn_in.astype(BF16), gdn_out=gdn_w_out.astype(BF16),
        mu=rwkv_mu.reshape(nl, 1, A_SHIFT_COLS),
        rwkv_vecs=jnp.stack([rwkv_w0, rwkv_a0, rwkv_k_k, rwkv_k_a, rwkv_r_k.reshape(nl, C_A), rwkv_gn_g, rwkv_gn_b,
                             jnp.zeros_like(rwkv_w0)], axis=1),
        lora_w=lora_w.astype(BF16),
        ones_blk=jnp.asarray((lane_head[:, None] == lane_head[None, :]).astype(np.float32)),
        ret_g=ret_gn_g.reshape(nl, 1, C_B), ret_b=ret_gn_b.reshape(nl, 1, C_B),
        conv_w=gdn_conv_w, a_log=_lane_pad(gdn_A_log, LANES)[:, None, :], dt_bias=_lane_pad(gdn_dt_bias, LANES)[:, None, :],
        norm_g=gdn_norm_g[:, None, :],
    )

    cfg_p = (SeqCfg(1, 64, 64, seq // 64, 0, False, UNITS_DEEP),
             SeqCfg(1, 128, 128, seq // 128, 0, False, UNITS),
             SeqCfg(1, 64, 64, seq // 64, 0, False, UNITS_DEEP))
    cfg = SeqCfg(SEQ_PER_UNIT, SAMPLE_PAD, dec_seq, 1, PAST_LEN, True, UNITS)
    xs = jnp.pad(x_sample, ((0, 0), (0, SAMPLE_PAD - dec_seq), (0, 0))).reshape(ndec * SAMPLE_PAD, D_MODEL)
    st = dict(shift=_history_tiles(state_rwkv_shift[:, :, None, :]),
              rwkv=state_rwkv.transpose(0, 1, 3, 2, 4).reshape(nl, ndec, N_A, C_A),
              ret=state_ret,
              conv=_history_tiles(state_gdn_conv),
              gdn=state_gdn)
    cfg_w = SeqCfg(SEQ_PER_UNIT, SAMPLE_PAD, dec_seq, 1, PAST_LEN, True, 2 * UNITS)
    (hp, hs), (sp, ss) = _trunk([x_prompt.reshape(nb * seq, D_MODEL), xs], [cfg_p, (cfg_w, cfg_w, cfg)], [None, st], p)

    def unpack(s, n):
        return (s["shift"].reshape(nl, n, SUBLANES, A_SHIFT_COLS)[:, :, SUBLANES - 1],
                s["rwkv"].reshape(nl, n, N_A, H_A, N_A).transpose(0, 1, 3, 2, 4),
                s["ret"],
                s["conv"].reshape(N_C_LAYERS, n, SUBLANES, N_QKV_C)[:, :, SUBLANES - (CONV_W - 1):],
                s["gdn"])

    y_prompt = hp.reshape(nb, seq, D_MODEL)
    y_sample = hs.reshape(ndec, SAMPLE_PAD, D_MODEL)[:, :dec_seq]
    return (y_prompt, y_sample) + unpack(sp, nb) + unpack(ss, ndec)
```
